```python
import math
import jax
import jax.numpy as jnp
from jax import lax
import numpy as np

D_MODEL = 1024
BATCH = 4
SEQ = 8192
DEPTH = 2

CTX_LEN = 256
GRID_W = 64
N_MIXERS = 4
GROUP_WIDTH = D_MODEL // N_MIXERS
HEAD_DIM = 64
N_HEADS = GROUP_WIDTH // HEAD_DIM
CHUNK = 64
CONV_K = 3
GLA_GATE_RANK = 16
GLA_GATE_NORM = 16.0
RWKV_DECAY_RANK = 32
RWKV_A_RANK = 32
RWKV_G_RANK = 64
RWKV_GN_EPS = 64e-5
SSM_STATE = 128
SSM_GROUPS = 2
SSM_HPG = N_HEADS // SSM_GROUPS
SSM_XBC = GROUP_WIDTH + 2 * SSM_GROUPS * SSM_STATE
N_EXPERTS = 64
TOP_K = 8
EXPERT_DIM = 256
SHARED_DIM = 256
ROUTED_SCALE = 2.5
MOE_BLOCK = 128
RMS_EPS = 1e-6

IN_SPLITS = (
    GROUP_WIDTH, GROUP_WIDTH, GROUP_WIDTH, GROUP_WIDTH, GLA_GATE_RANK, GLA_GATE_RANK,
    GROUP_WIDTH, GROUP_WIDTH, GROUP_WIDTH, RWKV_DECAY_RANK, RWKV_DECAY_RANK, RWKV_A_RANK, RWKV_G_RANK,
    GROUP_WIDTH, SSM_XBC, N_HEADS, N_HEADS,
    3 * GROUP_WIDTH, GROUP_WIDTH, N_HEADS, N_HEADS, N_HEADS, N_HEADS,
)
IN_WIDTH = sum(IN_SPLITS)

kernel_name = "hymba_style_bidir_hybrid_moe_block"


def rms_norm(x, g):
    xf = x.astype(jnp.float32)
    y = xf * lax.rsqrt(jnp.mean(xf * xf, axis=-1, keepdims=True) + RMS_EPS)
    return (y * g.astype(jnp.float32)).astype(x.dtype)


def modulate(x, g, shift, scale):
    return rms_norm(x, g) * (1 + scale) + shift


def split_heads(t, n=N_HEADS):
    return t.reshape(t.shape[:-1] + (n, t.shape[-1] // n)).astype(jnp.float32)


def merge_heads(t):
    return t.reshape(t.shape[:-2] + (-1,))


def l2norm(t, eps=1e-6):
    return t * lax.rsqrt(jnp.sum(t * t, axis=-1, keepdims=True) + eps)


def head_rms(o, g):
    o = o * lax.rsqrt(jnp.mean(o * o, axis=-1, keepdims=True) + RMS_EPS)
    return merge_heads(o * g.astype(jnp.float32).reshape(o.shape[-2:]))


def flip(t):
    return t[:, ::-1]


def to_chunks(t):
    b, l = t.shape[:2]
    t = t.reshape((b, l // CHUNK, CHUNK) + t.shape[2:])
    return jnp.swapaxes(t, 2, 3)


def from_chunks(t):
    t = jnp.swapaxes(t, 2, 3)
    return t.reshape((t.shape[0], t.shape[1] * t.shape[2]) + t.shape[3:])


def bidir(core, fwd_args, bwd_args, init):
    y_f, s_f = core(*fwd_args, init[0])
    y_b, s_b = core(*[flip(a) for a in bwd_args], init[1])
    return y_f + flip(y_b), (s_f, s_b)


def dwconv2d(u, w, rows, cols):
    b, l, ch = u.shape
    y = lax.conv_general_dilated(
        u.reshape(b, rows, cols, ch), w[:, :, None, :].astype(u.dtype), (1, 1), 'SAME',
        dimension_numbers=('NHWC', 'HWIO', 'NHWC'), feature_group_count=ch)
    return y.reshape(b, l, ch)


def zero_states(batch):
    z = lambda *s: jnp.zeros((batch, N_HEADS) + s, jnp.float32)
    pair = lambda *s: (z(*s), z(*s))
    return (pair(HEAD_DIM, HEAD_DIM), pair(HEAD_DIM, HEAD_DIM),
            pair(HEAD_DIM, SSM_STATE), pair(HEAD_DIM, HEAD_DIM))


def gla_core(q, k, v, gk, s0):
    incl = jnp.tril(jnp.ones((CHUNK, CHUNK), bool))
    q, k, v, gk = (to_chunks(t) for t in (q * q.shape[-1] ** -0.5, k, v, gk))
    b = jnp.cumsum(gk, axis=3)
    b_last = b[:, :, :, -1:]
    q_t = q * jnp.exp(b)
    k_t = k * jnp.exp(-b)
    k_s = k * jnp.exp(b_last - b)
    att = jnp.where(incl, jnp.einsum('bnhid,bnhjd->bnhij', q_t, k_t), 0.0)
    o_intra = jnp.einsum('bnhij,bnhjv->bnhiv', att, v)

    def step(s, xs):
        q_n, k_n, v_n, dec_n = xs
        o_n = jnp.einsum('bhid,bhdv->bhiv', q_n, s)
        s = s * dec_n[..., None] + jnp.einsum('bhjd,bhjv->bhdv', k_n, v_n)
        return s, o_n

    xs = tuple(jnp.moveaxis(t, 1, 0) for t in (q_t, k_s, v, jnp.exp(b_last[:, :, :, 0])))
    s_fin, o_inter = lax.scan(step, s0, xs)
    return from_chunks(o_intra + jnp.moveaxis(o_inter, 0, 1)), s_fin


def gla_mixer(q, k, v, g, gl_f, gl_b, lp, init):
    dt_in = g.dtype
    q, k, v = split_heads(q), split_heads(k), split_heads(v)

    def forget(gl, d):
        z = gl @ lp['gla_gate_w2'][d] + lp['gla_gate_b'][d]
        return split_heads(jax.nn.log_sigmoid(z.astype(jnp.float32))) / GLA_GATE_NORM

    o, st = bidir(gla_core, (q, k, v, forget(gl_f, 0)), (q, k, v, forget(gl_b, 1)), init)
    y = head_rms(o, lp['gla_norm_g']) * jax.nn.silu(g.astype(jnp.float32))
    return y.astype(dt_in), st


def rwkv_core(r, w, k, v, z, b, s0):
    def step(s, xs):
        r_t, w_t, k_t, v_t, z_t, b_t = xs
        sz = jnp.einsum('bhvk,bhk->bhv', s, z_t)
        s = (s * w_t[:, :, None, :] + sz[..., None] * b_t[:, :, None, :]
             + v_t[..., None] * k_t[:, :, None, :])
        return s, jnp.einsum('bhvk,bhk->bhv', s, r_t)

    s_fin, y = lax.scan(step, s0, tuple(jnp.moveaxis(t, 1, 0) for t in (r, w, k, v, z, b)))
    return jnp.moveaxis(y, 0, 1), s_fin


def rwkv_mixer(r, k, v, wl_f, wl_b, al, gl, lp, init):
    f32 = jnp.float32
    dt_in = r.dtype
    r, k, v = split_heads(r), split_heads(k), split_heads(v)

    def decay(wl, d):
        w = lp['rwkv_w0'][d] + jnp.tanh(wl) @ lp['rwkv_w_w2'][d]
        w = -jax.nn.softplus(-w.astype(f32)) - 0.5
        return split_heads(jnp.exp(-jnp.exp(w)))

    a = split_heads(jax.nn.sigmoid((lp['rwkv_a0'] + al @ lp['rwkv_a_w2']).astype(f32)))
    gate = (jax.nn.sigmoid(gl) @ lp['rwkv_g_w2']).astype(f32)
    kk = l2norm(k * split_heads(lp['rwkv_k_k']))
    k = k * (1.0 + (a - 1.0) * split_heads(lp['rwkv_k_a']))
    z, bvec = -kk, kk * a
    o, st = bidir(rwkv_core, (r, decay(wl_f, 0), k, v, z, bvec),
                  (r, decay(wl_b, 1), k, v, z, bvec), init)
    mu = jnp.mean(o, axis=-1, keepdims=True)
    var = jnp.mean(jnp.square(o - mu), axis=-1, keepdims=True)
    o = merge_heads((o - mu) * lax.rsqrt(var + RWKV_GN_EPS))
    o = o * lp['rwkv_ln_g'].astype(f32) + lp['rwkv_ln_b'].astype(f32)
    bonus = jnp.sum(r * k * lp['rwkv_r_k'].astype(f32), axis=-1, keepdims=True) * v
    y = (o + merge_heads(bonus)) * gate
    return y.astype(dt_in), st


def ssd_core(x, a, bm, cm, s0):
    incl = jnp.tril(jnp.ones((CHUNK, CHUNK), bool))
    xc, ac, bc, cc = (to_chunks(t) for t in (x, a, bm, cm))
    bn, nc = xc.shape[:2]
    acum = jnp.cumsum(ac, axis=-1)
    seg = acum[..., :, None] - acum[..., None, :]
    lmat = jnp.exp(jnp.where(incl, seg, -jnp.inf))
    cb = jnp.einsum('bngis,bngjs->bngij', cc, bc)
    xg = xc.reshape(bn, nc, SSM_GROUPS, SSM_HPG, CHUNK, HEAD_DIM)
    m = cb[:, :, :, None] * lmat.reshape(bn, nc, SSM_GROUPS, SSM_HPG, CHUNK, CHUNK)
    y_intra = jnp.einsum('bngkij,bngkjp->bngkip', m, xg)
    dte = jnp.exp(acum[..., -1:] - acum).reshape(bn, nc, SSM_GROUPS, SSM_HPG, CHUNK)
    cs = jnp.einsum('bngjs,bngkj,bngkjp->bngkps', bc, dte, xg)
    cs = cs.reshape(bn, nc, N_HEADS, HEAD_DIM, SSM_STATE)
    cd = jnp.exp(acum[..., -1])

    def step(s, xs):
        cs_n, cd_n = xs
        return s * cd_n[..., None, None] + cs_n, s

    s_fin, s_in = lax.scan(step, s0, (jnp.moveaxis(cs, 1, 0), jnp.moveaxis(cd, 1, 0)))
    s_in = jnp.moveaxis(s_in, 0, 1).reshape(bn, nc, SSM_GROUPS, SSM_HPG, HEAD_DIM, SSM_STATE)
    dec_in = jnp.exp(acum).reshape(bn, nc, SSM_GROUPS, SSM_HPG, CHUNK)
    y_off = jnp.einsum('bngis,bngkps->bngkip', cc, s_in) * dec_in[..., None]
    y = (y_intra + y_off).reshape(bn, nc, N_HEADS, CHUNK, HEAD_DIM)
    return from_chunks(y), s_fin


def mamba_mixer(z, xbc, dt_f, dt_b, lp, init, rows, cols):
    f32 = jnp.float32
    dt_in = z.dtype
    xbc = jax.nn.silu(dwconv2d(xbc, lp['ssm_conv_w'], rows, cols) + lp['ssm_conv_b'])
    xs, bm, cm = jnp.split(xbc, [GROUP_WIDTH, GROUP_WIDTH + SSM_GROUPS * SSM_STATE], axis=-1)
    xs = split_heads(xs)
    bm, cm = split_heads(bm, SSM_GROUPS), split_heads(cm, SSM_GROUPS)

    def discretise(dt_raw, d):
        dt = jax.nn.softplus((dt_raw + lp['ssm_dt_bias'][d]).astype(f32))
        return xs * dt[..., None], -jnp.exp(lp['ssm_A_log'][d].astype(f32)) * dt

    x_f, a_f = discretise(dt_f, 0)
    x_b, a_b = discretise(dt_b, 1)
    y, st = bidir(ssd_core, (x_f, a_f, bm, cm), (x_b, a_b, bm, cm), init)
    y = y + lp['ssm_D'].astype(f32)[:, None] * xs
    y = merge_heads(y) * jax.nn.silu(z.astype(f32))
    y = y.reshape(y.shape[:-1] + (SSM_GROUPS, -1))
    y = y * lax.rsqrt(jnp.mean(y * y, axis=-1, keepdims=True) + RMS_EPS)
    y = y.reshape(y.shape[:-2] + (GROUP_WIDTH,)) * lp['ssm_norm_g'].astype(f32)
    return y.astype(dt_in), st


def gdn_core(q, k, v, g, beta, s0):
    incl = jnp.tril(jnp.ones((CHUNK, CHUNK), bool))
    strict = jnp.tril(jnp.ones((CHUNK, CHUNK), bool), -1)
    q, k, v, g, beta = (to_chunks(t) for t in (q, k, v, g, beta))
    gcum = jnp.cumsum(g, axis=-1)
    seg = gcum[..., :, None] - gcum[..., None, :]
    kb = k * beta[..., None]
    a_kk = jnp.einsum('bnhid,bnhjd->bnhij', kb, k) * jnp.exp(jnp.where(strict, seg, -jnp.inf))
    rhs = jnp.concatenate([kb * jnp.exp(gcum)[..., None], v * beta[..., None]], axis=-1)
    sol = lax.linalg.triangular_solve(a_kk, rhs, left_side=True, lower=True, unit_diagonal=True)
    dk = k.shape[-1]
    w, u = sol[..., :dk], sol[..., dk:]
    a_qk = jnp.einsum('bnhid,bnhjd->bnhij', q, k) * jnp.exp(jnp.where(incl, seg, -jnp.inf))
    q_dec = q * jnp.exp(gcum)[..., None]
    k_dec = k * jnp.exp(gcum[..., -1:] - gcum)[..., None]
    g_last = jnp.exp(gcum[..., -1])

    def step(s, xs):
        w_n, u_n, aqk_n, q_n, k_n, gl_n = xs
        v_new = u_n - w_n @ s
        o_n = q_n @ s + aqk_n @ v_new
        s = s * gl_n[..., None, None] + jnp.swapaxes(k_n, -1, -2) @ v_new
        return s, o_n

    xs = tuple(jnp.moveaxis(t, 1, 0) for t in (w, u, a_qk, q_dec, k_dec, g_last))
    s_fin, o = lax.scan(step, s0, xs)
    return from_chunks(jnp.moveaxis(o, 0, 1)), s_fin


def deltanet_mixer(qkv, z, b_f, b_b, a_f, a_b, lp, init, rows, cols):
    f32 = jnp.float32
    dt_in = z.dtype
    qkv = jax.nn.silu(dwconv2d(qkv, lp['dn_conv_w'], rows, cols))
    q, k, v = (split_heads(t) for t in jnp.split(qkv, 3, axis=-1))
    q = l2norm(q) * HEAD_DIM ** -0.5
    k = l2norm(k)

    def gates(b_raw, a_raw, d):
        beta = jax.nn.sigmoid(b_raw.astype(f32))
        g = -jnp.exp(lp['dn_A_log'][d].astype(f32)) * jax.nn.softplus(
            (a_raw + lp['dn_dt_bias'][d]).astype(f32))
        return g, beta

    g_f, be_f = gates(b_f, a_f, 0)
    g_b, be_b = gates(b_b, a_b, 1)
    o, st = bidir(gdn_core, (q, k, v, g_f, be_f), (q, k, v, g_b, be_b), init)
    y = head_rms(o, lp['dn_norm_g']) * jax.nn.silu(z.astype(f32))
    return y.astype(dt_in), st


def mixer_block(h, init, lp, rows, cols):
    cuts = np.cumsum(IN_SPLITS)[:-1].tolist()
    (aq, ak, av, ag, agf, agb,
     br, bk, bv, bwf, bwb, ba, bg,
     cz, cxbc, cdf, cdb,
     dqkv, dz, dbf, dbb, daf, dab) = jnp.split(h @ lp['w_in'], cuts, axis=-1)
    ya, sa = gla_mixer(aq, ak, av, ag, agf, agb, lp, init[0])
    yb, sb = rwkv_mixer(br, bk, bv, bwf, bwb, ba, bg, lp, init[1])
    yc, sc = mamba_mixer(cz, cxbc, cdf, cdb, lp, init[2], rows, cols)
    yd, sd = deltanet_mixer(dqkv, dz, dbf, dbb, daf, dab, lp, init[3], rows, cols)
    y = jnp.concatenate([ya, yb, yc, yd], axis=-1) @ lp['w_out']
    return y, (sa, sb, sc, sd)


def swiglu(t, wg, wu, wd):
    return (jax.nn.silu(t @ wg) * (t @ wu)) @ wd


def moe_ffn(t, router_w, router_bias, w_gate, w_up, w_down, sh_gate, sh_up, sh_down):
    n_tok, d = t.shape
    n_assign = n_tok * TOP_K
    scores = jax.nn.sigmoid((t @ router_w).astype(jnp.float32))
    _, idx = lax.top_k(scores + router_bias.astype(jnp.float32), TOP_K)
    wts = jnp.take_along_axis(scores, idx, axis=-1)
    wts = wts / jnp.sum(wts, axis=-1, keepdims=True) * ROUTED_SCALE
    e_flat = idx.reshape(-1)
    order = jnp.argsort(e_flat)
    e_sorted = e_flat[order]
    tok_sorted = (order // TOP_K).astype(jnp.int32)
    w_sorted = wts.reshape(-1)[order]
    counts = jnp.bincount(e_flat, length=N_EXPERTS)
    padded = (counts + MOE_BLOCK - 1) // MOE_BLOCK * MOE_BLOCK
    start = jnp.cumsum(counts) - counts
    pend = jnp.cumsum(padded)
    pstart = pend - padded
    dest = pstart[e_sorted] + jnp.arange(n_assign) - start[e_sorted]
    n_blocks = -(-(n_assign + N_EXPERTS * (MOE_BLOCK - 1)) // MOE_BLOCK)
    buf_tok = jnp.zeros((n_blocks * MOE_BLOCK,), jnp.int32).at[dest].set(tok_sorted)
    buf_w = jnp.zeros((n_blocks * MOE_BLOCK,), jnp.float32).at[dest].set(w_sorted)
    blk_exp = jnp.minimum(
        jnp.searchsorted(pend, jnp.arange(n_blocks) * MOE_BLOCK, side='right'), N_EXPERTS - 1)

    def expert_block(args):
        tok, e = args
        return swiglu(t[tok], w_gate[e], w_up[e], w_down[e])

    y = lax.map(expert_block, (buf_tok.reshape(n_blocks, MOE_BLOCK), blk_exp))
    y = y.reshape(-1, d).astype(jnp.float32) * buf_w[:, None]
    routed = jnp.zeros((n_tok, d), jnp.float32).at[buf_tok].add(y)
    shared = swiglu(t, sh_gate, sh_up, sh_down).astype(jnp.float32)
    return (routed + shared).astype(t.dtype)


def setup_inputs(seed: int = 0) -> dict:
    key = jax.random.key(seed)
    keys = iter(jax.random.split(key, 48))
    f32 = jnp.float32

    def nrm(shape, scale):
        return jax.random.normal(next(keys), shape, f32) * scale

    def gain(shape):
        return 1.0 + nrm(shape, 0.05)

    def dt_bias(shape):
        dt = jnp.exp(jax.random.uniform(next(keys), shape, f32, math.log(1e-3), math.log(1e-1)))
        return dt + jnp.log(-jnp.expm1(-dt))

    def a_log(shape):
        return jnp.log(jax.random.uniform(next(keys), shape, f32, 1.0, 16.0))

    L, D, G = DEPTH, D_MODEL, GROUP_WIDTH
    return {
        'x': nrm((BATCH, SEQ, D), 1.0),
        'c': nrm((BATCH, D), 1.0),
        'ctx': nrm((BATCH, CTX_LEN, D), 1.0),
        'c_ctx': nrm((D,), 1.0),
        'norm1_g': gain((L, D)),
        'norm2_g': gain((L, D)),
        'w_mod': nrm((L, D, 6 * D), 0.5 * D ** -0.5),
        'b_mod': nrm((L, 6 * D), 0.02),
        'w_in': nrm((L, D, IN_WIDTH), D ** -0.5),
        'w_out': nrm((L, D, D), D ** -0.5),
        'gla_gate_w2': nrm((L, 2, GLA_GATE_RANK, G), GLA_GATE_RANK ** -0.5),
        'gla_gate_b': nrm((L, 2, G), 0.1),
        'gla_norm_g': gain((L, G)),
        'rwkv_w_w2': nrm((L, 2, RWKV_DECAY_RANK, G), 0.1),
        'rwkv_w0': -2.0 + nrm((L, 2, G), 1.0),
        'rwkv_a_w2': nrm((L, RWKV_A_RANK, G), RWKV_A_RANK ** -0.5),
        'rwkv_a0': nrm((L, G), 0.1),
        'rwkv_g_w2': nrm((L, RWKV_G_RANK, G), RWKV_G_RANK ** -0.5),
        'rwkv_k_k': 0.85 + nrm((L, G), 0.05),
        'rwkv_k_a': gain((L, G)),
        'rwkv_r_k': nrm((L, N_HEADS, HEAD_DIM), 0.1),
        'rwkv_ln_g': gain((L, G)),
        'rwkv_ln_b': nrm((L, G), 0.02),
        'ssm_conv_w': nrm((L, CONV_K, CONV_K, SSM_XBC), 1.0 / CONV_K),
        'ssm_conv_b': nrm((L, SSM_XBC), 0.02),
        'ssm_A_log': a_log((L, 2, N_HEADS)),
        'ssm_dt_bias': dt_bias((L, 2, N_HEADS)),
        'ssm_D': gain((L, N_HEADS)),
        'ssm_norm_g': gain((L, G)),
        'dn_conv_w': nrm((L, CONV_K, CONV_K, 3 * G), 1.0 / CONV_K),
        'dn_A_log': a_log((L, 2, N_HEADS)),
        'dn_dt_bias': dt_bias((L, 2, N_HEADS)),
        'dn_norm_g': gain((L, G)),
        'router_w': nrm((L, D, N_EXPERTS), D ** -0.5),
        'router_bias': nrm((L, N_EXPERTS), 0.01),
        'exp_w_gate': nrm((L, N_EXPERTS, D, EXPERT_DIM), D ** -0.5),
        'exp_w_up': nrm((L, N_EXPERTS, D, EXPERT_DIM), D ** -0.5),
        'exp_w_down': nrm((L, N_EXPERTS, EXPERT_DIM, D), EXPERT_DIM ** -0.5),
        'sh_w_gate': nrm((L, D, SHARED_DIM), D ** -0.5),
        'sh_w_up': nrm((L, D, SHARED_DIM), D ** -0.5),
        'sh_w_down': nrm((L, SHARED_DIM, D), SHARED_DIM ** -0.5),
        'final_norm_g': gain((D,)),
    }


def reference(x, c, ctx, c_ctx, norm1_g, norm2_g, w_mod, b_mod, w_in, w_out,
              gla_gate_w2, gla_gate_b, gla_norm_g,
              rwkv_w_w2, rwkv_w0, rwkv_a_w2, rwkv_a0, rwkv_g_w2, rwkv_k_k, rwkv_k_a,
              rwkv_r_k, rwkv_ln_g, rwkv_ln_b,
              ssm_conv_w, ssm_conv_b, ssm_A_log, ssm_dt_bias, ssm_D, ssm_norm_g,
              dn_conv_w, dn_A_log, dn_dt_bias, dn_norm_g,
              router_w, router_bias, exp_w_gate, exp_w_up, exp_w_down,
              sh_w_gate, sh_w_up, sh_w_down, final_norm_g):
    bsz, n_lat, d = x.shape
    rows = n_lat // GRID_W
    n_ctx = ctx.shape[1]
    ctx_init = zero_states(bsz)
    for i in range(DEPTH):
        last = i == DEPTH - 1
        lp = dict(
            w_in=w_in[i], w_out=w_out[i],
            gla_gate_w2=gla_gate_w2[i], gla_gate_b=gla_gate_b[i], gla_norm_g=gla_norm_g[i],
            rwkv_w_w2=rwkv_w_w2[i], rwkv_w0=rwkv_w0[i], rwkv_a_w2=rwkv_a_w2[i],
            rwkv_a0=rwkv_a0[i], rwkv_g_w2=rwkv_g_w2[i], rwkv_k_k=rwkv_k_k[i],
            rwkv_k_a=rwkv_k_a[i], rwkv_r_k=rwkv_r_k[i], rwkv_ln_g=rwkv_ln_g[i],
            rwkv_ln_b=rwkv_ln_b[i],
            ssm_conv_w=ssm_conv_w[i], ssm_conv_b=ssm_conv_b[i], ssm_A_log=ssm_A_log[i],
            ssm_dt_bias=ssm_dt_bias[i], ssm_D=ssm_D[i], ssm_norm_g=ssm_norm_g[i],
            dn_conv_w=dn_conv_w[i], dn_A_log=dn_A_log[i], dn_dt_bias=dn_dt_bias[i],
            dn_norm_g=dn_norm_g[i])
        mod_x = jax.nn.silu(c) @ w_mod[i] + b_mod[i]
        mod_c = jax.nn.silu(c_ctx) @ w_mod[i] + b_mod[i]
        sa_x, ca_x, ga_x, sf_x, cf_x, gf_x = jnp.split(mod_x[:, None, :], 6, axis=-1)
        sa_c, ca_c, ga_c, sf_c, cf_c, gf_c = jnp.split(mod_c[None, None, :], 6, axis=-1)
        y_c, ctx_states = mixer_block(modulate(ctx, norm1_g[i], sa_c, ca_c), ctx_init, lp, 1, n_ctx)
        y_x, _ = mixer_block(modulate(x, norm1_g[i], sa_x, ca_x), ctx_states, lp, rows, GRID_W)
        x = x + ga_x * y_x
        h_x = modulate(x, norm2_g[i], sf_x, cf_x).reshape(-1, d)
        moe_args = (router_w[i], router_bias[i], exp_w_gate[i], exp_w_up[i], exp_w_down[i],
                    sh_w_gate[i], sh_w_up[i], sh_w_down[i])
        if last:
            x = x + gf_x * moe_ffn(h_x, *moe_args).reshape(x.shape)
        else:
            ctx = ctx + ga_c * y_c
            h_c = modulate(ctx, norm2_g[i], sf_c, cf_c).reshape(-1, d)
            f = moe_ffn(jnp.concatenate([h_c, h_x], axis=0), *moe_args)
            ctx = ctx + gf_c * f[: bsz * n_ctx].reshape(ctx.shape)
            x = x + gf_x * f[bsz * n_ctx:].reshape(x.shape)
    return rms_norm(x, final_norm_g)
```

```python
import functools

import jax
import jax.numpy as jnp
from jax import lax
from jax.experimental import pallas as pl
from jax.experimental.pallas import tpu as pltpu

F32 = jnp.float32
BF16 = jnp.bfloat16

D_MODEL = 1024
GROUP = 256
N_HEADS = 4
HEAD_DIM = 64
CHUNK = 64
GRID_W = 64
SSM_STATE = 128
N_EXPERTS = 64
TOP_K = 8
EXPERT_DIM = 256
ROUTED_SCALE = 2.5
RMS_EPS = 1e-6
RWKV_GN_EPS = 64e-5
GLA_GATE_NORM = 16.0
LANE = 128
CONV_HALO = 128
VMEM_LIMIT = 56 * 1024 * 1024

WA = 4 * GROUP + LANE
WB = 4 * GROUP
WC = 4 * GROUP + LANE
WD = 4 * GROUP + LANE
W_IN_PAD = WA + WB + WC + WD

NT = (((1,), (1,)), ((), ()))
TN = (((0,), (0,)), ((), ()))


def _dot(a, b):
    return jnp.dot(a, b, preferred_element_type=F32)


def _dot_nt(a, b):
    return lax.dot_general(a, b, NT, preferred_element_type=F32)


def _dot_tn(a, b):
    return lax.dot_general(a, b, TN, preferred_element_type=F32)


def _split3(x):
    hi = x.astype(BF16)
    r1 = x - hi.astype(F32)
    mid = r1.astype(BF16)
    lo = (r1 - mid.astype(F32)).astype(BF16)
    return hi, mid, lo


def _dot_sel_l(sel, x):
    hi, mid, lo = _split3(x)
    return _dot(sel, hi) + _dot(sel, mid) + _dot(sel, lo)


def _dot_sel_r(x, sel):
    hi, mid, lo = _split3(x)
    return _dot(hi, sel) + _dot(mid, sel) + _dot(lo, sel)


def _dot3(a, b):
    ah, am, _ = _split3(a)
    bh, bm, _ = _split3(b)
    return _dot(ah, bh) + (_dot(ah, bm) + _dot(am, bh))


def _softplus(x):
    return jnp.maximum(x, 0.0) + jnp.log1p(jnp.exp(-jnp.abs(x)))


def _sigmoid(x):
    return 1.0 / (1.0 + jnp.exp(-x))


def _silu(x):
    return x * _sigmoid(x)


def _head_ones():
    r = lax.broadcasted_iota(jnp.int32, (GROUP, GROUP), 0) // HEAD_DIM
    c = lax.broadcasted_iota(jnp.int32, (GROUP, GROUP), 1) // HEAD_DIM
    return jnp.where(r == c, 1.0, 0.0).astype(BF16)


def _head_sum(x):
    return _dot_sel_r(x, _head_ones())


def _scan_masks(direction):
    r = lax.broadcasted_iota(jnp.int32, (CHUNK, CHUNK), 0)
    c = lax.broadcasted_iota(jnp.int32, (CHUNK, CHUNK), 1)
    diff = (r - c) * (1 - 2 * direction)
    return diff >= 0, diff > 0, r == c


def _neumann_inverse(n, eye):
    t = eye + n
    p = n
    for _ in range(5):
        p = _dot(p, p)
        t = t + _dot(t, p)
    return t


def _chunk_index(n, direction, n_chunks):
    return n + direction * (n_chunks - 1 - 2 * n)


def _params(*sem):
    return pltpu.CompilerParams(dimension_semantics=sem, vmem_limit_bytes=VMEM_LIMIT)


def _mod_kernel(c_ref, w_ref, b_ref, o_ref):
    o_ref[...] = _dot3(_silu(c_ref[...]), w_ref[...]) + b_ref[...]


def _modulation(c_rows, w, b):
    m, d = c_rows.shape
    n = w.shape[1]
    tn = 1536
    return pl.pallas_call(
        _mod_kernel,
        grid=(n // tn,),
        in_specs=[pl.BlockSpec((m, d), lambda j: (0, 0)),
                  pl.BlockSpec((d, tn), lambda j: (0, j)),
                  pl.BlockSpec((1, tn), lambda j: (0, j))],
        out_specs=pl.BlockSpec((m, tn), lambda j: (0, j)),
        out_shape=jax.ShapeDtypeStruct((m, n), F32),
        compiler_params=_params("parallel"),
        name="modulation",
    )(c_rows, w, b.reshape(1, n))


def _modulated_norm(x, g, shift, scale):
    y = x * lax.rsqrt(jnp.mean(x * x, axis=-1, keepdims=True) + RMS_EPS)
    return y * g * (1.0 + scale) + shift


def _inproj_kernel(x_ref, g_ref, sh_ref, sc_ref, w_ref, oa_ref, ob_ref, oc_ref, od_ref):
    h = _modulated_norm(x_ref[...], g_ref[...], sh_ref[...], sc_ref[...]).astype(BF16)
    y = _dot(h, w_ref[...])
    oa_ref[...] = y[:, :WA]
    ob_ref[...] = y[:, WA:WA + WB]
    oc_ref[...] = y[:, WA + WB:WA + WB + WC]
    od_ref[...] = y[:, WA + WB + WC:]


def _in_projection(x, g, shift, scale, w_pad):
    bsz, seq, d = x.shape
    tm = 256
    vec = pl.BlockSpec((None, 1, d), lambda b, i: (b, 0, 0))
    out = lambda w: pl.BlockSpec((None, tm, w), lambda b, i: (b, i, 0))
    shp = lambda w: jax.ShapeDtypeStruct((bsz, seq, w), F32)
    return pl.pallas_call(
        _inproj_kernel,
        grid=(bsz, seq // tm),
        in_specs=[pl.BlockSpec((None, tm, d), lambda b, i: (b, i, 0)),
                  pl.BlockSpec((1, d), lambda b, i: (0, 0)),
                  vec, vec,
                  pl.BlockSpec((d, W_IN_PAD), lambda b, i: (0, 0))],
        out_specs=[out(WA), out(WB), out(WC), out(WD)],
        out_shape=[shp(WA), shp(WB), shp(WC), shp(WD)],
        compiler_params=_params("parallel", "parallel"),
        name="in_projection",
    )(x, g.reshape(1, d), shift, scale, w_pad)


def _conv_kernel(prev_ref, cur_ref, next_ref, w_ref, b_ref, o_ref, ext_ref, *, cols, taps_r, n_tiles):
    i = pl.program_id(1)
    tm, ch = cur_ref.shape
    ext_ref[CONV_HALO:CONV_HALO + tm, :] = cur_ref[...]
    ext_ref[:CONV_HALO, :] = jnp.where(i > 0, prev_ref[...], 0.0)
    ext_ref[CONV_HALO + tm:, :] = jnp.where(i < n_tiles - 1, next_ref[...], 0.0)
    rt = 128
    for r0 in range(0, tm, rt):
        col = (lax.broadcasted_iota(jnp.int32, (rt, LANE), 0) + r0) % cols
        for c0 in range(0, ch, LANE):
            acc = jnp.zeros((rt, LANE), F32) + b_ref[:, c0:c0 + LANE]
            for dr in taps_r:
                for dc in (-1, 0, 1):
                    off = CONV_HALO + r0 + dr * cols + dc
                    tap = ext_ref[off:off + rt, c0:c0 + LANE]
                    if dc != 0:
                        tap = jnp.where((col + dc >= 0) & (col + dc < cols), tap, 0.0)
                    k = (dr + 1) * 3 + (dc + 1)
                    acc = acc + tap * w_ref[k:k + 1, c0:c0 + LANE]
            o_ref[r0:r0 + rt, c0:c0 + LANE] = _silu(acc)


def _grid_conv(proj, lane0, w, b, rows, cols):
    bsz, seq, _ = proj.shape
    tm = min(seq, 512)
    n_tiles = seq // tm
    hpt = tm // CONV_HALO
    n_halo = seq // CONV_HALO
    reach = cols + 1 if rows > 1 else 1
    assert seq == rows * cols and tm % cols == 0 and reach <= CONV_HALO and lane0 % GROUP == 0
    kern = functools.partial(_conv_kernel, cols=cols, taps_r=(-1, 0, 1) if rows > 1 else (0,),
                             n_tiles=n_tiles)
    w9 = w.reshape(9, 3 * GROUP)
    b1 = b.reshape(1, 3 * GROUP)
    outs = []
    for j in range(3):
        lb = lane0 // GROUP + j
        outs.append(pl.pallas_call(
            kern,
            grid=(bsz, n_tiles),
            in_specs=[
                pl.BlockSpec((None, CONV_HALO, GROUP),
                             lambda bi, i, lb=lb: (bi, jnp.maximum(i * hpt - 1, 0), lb)),
                pl.BlockSpec((None, tm, GROUP), lambda bi, i, lb=lb: (bi, i, lb)),
                pl.BlockSpec((None, CONV_HALO, GROUP),
                             lambda bi, i, lb=lb: (bi, jnp.minimum((i + 1) * hpt, n_halo - 1), lb)),
                pl.BlockSpec((9, GROUP), lambda bi, i, j=j: (0, j)),
                pl.BlockSpec((1, GROUP), lambda bi, i, j=j: (0, j))],
            out_specs=pl.BlockSpec((None, tm, GROUP), lambda bi, i: (bi, i, 0)),
            out_shape=jax.ShapeDtypeStruct((bsz, seq, GROUP), F32),
            scratch_shapes=[pltpu.VMEM((tm + 2 * CONV_HALO, GROUP), F32)],
            compiler_params=_params("parallel", "parallel"),
            name="grid_conv",
        )(proj, proj, proj, w9, b1))
    return outs


def _scan_prologue(s0_ref, s_scr):
    @pl.when(pl.program_id(2) == 0)
    def _():
        s_scr[...] = s0_ref[...]


def _scan_epilogue(sfin_ref, s_scr):
    @pl.when(pl.program_id(2) == pl.num_programs(2) - 1)
    def _():
        sfin_ref[...] = s_scr[...]


def _cum_rows_cols(g, incl_f, incl_t_f):
    cc = _dot_sel_l(incl_f, g)
    ones = jnp.ones((CHUNK, CHUNK), BF16)
    rr = _dot_sel_l(ones, g * jnp.concatenate([incl_t_f] * N_HEADS, axis=1))
    return cc, rr


def _gla_kernel(qkv_ref, tail_ref, w2_ref, b_ref, s0_ref, o_ref, sfin_ref, s_scr):
    direction = pl.program_id(1)
    _scan_prologue(s0_ref, s_scr)
    incl, _, _ = _scan_masks(direction)
    p = qkv_ref[...]
    q = p[:, :GROUP] * HEAD_DIM ** -0.5
    k = p[:, GROUP:2 * GROUP]
    v = p[:, 2 * GROUP:]
    zg = _dot(tail_ref[...], w2_ref[...]) + b_ref[...]
    gk = (jnp.minimum(zg, 0.0) - jnp.log1p(jnp.exp(-jnp.abs(zg)))) / GLA_GATE_NORM
    bc = _dot_sel_l(jnp.where(incl, 1.0, 0.0).astype(BF16), gk)
    bt = jnp.sum(gk, axis=0, keepdims=True)
    qt = q * jnp.exp(bc)
    kt = k * jnp.exp(-bc)
    ks = k * jnp.exp(bt - bc)
    dec = jnp.exp(bt)
    for h in range(N_HEADS):
        sl = slice(h * HEAD_DIM, (h + 1) * HEAD_DIM)
        st = s_scr[h]
        att = jnp.where(incl, _dot_nt(qt[:, sl], kt[:, sl]), 0.0)
        o_ref[:, sl] = _dot(att, v[:, sl]) + _dot_nt(qt[:, sl], st)
        s_scr[h] = st * dec[:, sl] + _dot_tn(v[:, sl], ks[:, sl])
    _scan_epilogue(sfin_ref, s_scr)


def _rwkv_kernel(p_ref, w0_ref, ww_ref, a0_ref, wa_ref, kk_ref, ka_ref, s0_ref, o_ref, sfin_ref, s_scr):
    direction = pl.program_id(1)
    _scan_prologue(s0_ref, s_scr)
    incl, strict, diag = _scan_masks(direction)
    eye = jnp.where(diag, 1.0, 0.0)
    p = p_ref[...]
    r = p[:, :GROUP]
    k = p[:, GROUP:2 * GROUP]
    v = p[:, 2 * GROUP:3 * GROUP]
    tail = p[:, 3 * GROUP:]
    wr = w0_ref[...] + _dot(jnp.tanh(tail), ww_ref[...])
    lw = -jnp.exp(-_softplus(-wr) - 0.5)
    a = _sigmoid(a0_ref[...] + _dot(tail, wa_ref[...]))
    kk = k * kk_ref[...]
    kk = kk * lax.rsqrt(_head_sum(kk * kk) + 1e-6)
    kmod = k * (1.0 + (a - 1.0) * ka_ref[...])
    zv = -kk
    bv = kk * a
    pc = _dot_sel_l(jnp.where(incl, 1.0, 0.0).astype(BF16), lw)
    pt = jnp.sum(lw, axis=0, keepdims=True)
    e_pc = jnp.exp(pc)
    e_npc = jnp.exp(-pc)
    e_rest = jnp.exp(pt - pc)
    zt = zv * jnp.exp(pc - lw)
    rt = r * e_pc
    bt = bv * e_npc
    kt = kmod * e_npc
    bh = bv * e_rest
    kh = kmod * e_rest
    dec = jnp.exp(pt)
    for h in range(N_HEADS):
        sl = slice(h * HEAD_DIM, (h + 1) * HEAD_DIM)
        st = s_scr[h]
        aa = _dot_nt(jnp.concatenate([zt[:, sl], rt[:, sl]], axis=0),
                     jnp.concatenate([bt[:, sl], kt[:, sl]], axis=0))
        a_zb = jnp.where(strict, aa[:CHUNK, :CHUNK], 0.0)
        a_zk = jnp.where(strict, aa[:CHUNK, CHUNK:], 0.0)
        a_rb = jnp.where(incl, aa[CHUNK:, :CHUNK], 0.0)
        a_rk = jnp.where(incl, aa[CHUNK:, CHUNK:], 0.0)
        t = _neumann_inverse(a_zb, eye)
        vh = v[:, sl]
        u = _dot_nt(_dot(t, zt[:, sl]), st) + _dot(t, _dot(a_zk, vh))
        o_ref[:, sl] = _dot_nt(rt[:, sl], st) + _dot(a_rb, u) + _dot(a_rk, vh)
        s_scr[h] = st * dec[:, sl] + _dot_tn(u, bh[:, sl]) + _dot_tn(vh, kh[:, sl])
    _scan_epilogue(sfin_ref, s_scr)


def _ssd_kernel(tail_ref, xs_ref, bm_ref, cm_ref, e_ref, dtb_ref, alog_ref, s0_ref,
                o_ref, sfin_ref, s_scr):
    direction = pl.program_id(1)
    _scan_prologue(s0_ref, s_scr)
    incl, _, _ = _scan_masks(direction)
    incl_f = jnp.where(incl, 1.0, 0.0)
    dt = _softplus(_dot_sel_r(tail_ref[...], e_ref[...]) + dtb_ref[...])
    a = -jnp.exp(alog_ref[...]) * dt
    x = xs_ref[...] * dt
    cc, rr = _cum_rows_cols(a, incl_f.astype(BF16), incl_f.T)
    at = jnp.sum(a, axis=0, keepdims=True)
    e_cc = jnp.exp(cc)
    xd = x * jnp.exp(at - cc)
    dec = jnp.exp(at)
    bm = bm_ref[...]
    cm = cm_ref[...]
    for g in range(2):
        gs = slice(g * SSM_STATE, (g + 1) * SSM_STATE)
        cb = _dot_nt(cm[:, gs], bm[:, gs])
        for h in range(2 * g, 2 * g + 2):
            sl = slice(h * HEAD_DIM, (h + 1) * HEAD_DIM)
            s = s_scr[h]
            lm = jnp.exp(jnp.where(incl, cc[:, sl] - rr[:, sl], -jnp.inf))
            o_ref[:, sl] = _dot(cb * lm, x[:, sl]) + _dot_nt(cm[:, gs], s) * e_cc[:, sl]
            s_scr[h] = s * dec[:, h * HEAD_DIM:h * HEAD_DIM + 1] + _dot_tn(xd[:, sl], bm[:, gs])
    _scan_epilogue(sfin_ref, s_scr)


def _gdn_kernel(tail_ref, q_ref, k_ref, v_ref, eb_ref, ea_ref, dtb_ref, alog_ref, s0_ref,
                o_ref, sfin_ref, s_scr):
    direction = pl.program_id(1)
    _scan_prologue(s0_ref, s_scr)
    incl, strict, diag = _scan_masks(direction)
    eye = jnp.where(diag, 1.0, 0.0)
    incl_f = jnp.where(incl, 1.0, 0.0)
    q = q_ref[...]
    q = q * lax.rsqrt(_head_sum(q * q) + 1e-6) * HEAD_DIM ** -0.5
    k = k_ref[...]
    k = k * lax.rsqrt(_head_sum(k * k) + 1e-6)
    v = v_ref[...]
    tail = tail_ref[...]
    beta = _sigmoid(_dot_sel_r(tail, eb_ref[...]))
    g = -jnp.exp(alog_ref[...]) * _softplus(_dot_sel_r(tail, ea_ref[...]) + dtb_ref[...])
    cc, rr = _cum_rows_cols(g, incl_f.astype(BF16), incl_f.T)
    gt = jnp.sum(g, axis=0, keepdims=True)
    kb = k * beta
    vb = v * beta
    kbe = kb * jnp.exp(cc)
    qd = q * jnp.exp(cc)
    kd = k * jnp.exp(gt - cc)
    dec = jnp.exp(gt)
    for h in range(N_HEADS):
        sl = slice(h * HEAD_DIM, (h + 1) * HEAD_DIM)
        s = s_scr[h]
        seg = cc[:, sl] - rr[:, sl]
        a_kk = _dot_nt(kb[:, sl], k[:, sl]) * jnp.exp(jnp.where(strict, seg, -jnp.inf))
        t = _neumann_inverse(-a_kk, eye)
        sol = _dot(t, jnp.concatenate([kbe[:, sl], vb[:, sl]], axis=1))
        w = sol[:, :HEAD_DIM]
        u = sol[:, HEAD_DIM:]
        a_qk = _dot_nt(q[:, sl], k[:, sl]) * jnp.exp(jnp.where(incl, seg, -jnp.inf))
        v_new = u - _dot(w, s)
        o_ref[:, sl] = _dot(qd[:, sl], s) + _dot(a_qk, v_new)
        s_scr[h] = s * dec[:, h * HEAD_DIM:h * HEAD_DIM + 1] + _dot_tn(kd[:, sl], v_new)
    _scan_epilogue(sfin_ref, s_scr)


def _scan_call(kern, name, data, dir_params, shared_params, init):
    bsz, seq = data[0][0].shape[:2]
    n_chunks = seq // CHUNK
    cidx = lambda n, d: _chunk_index(n, d, n_chunks)
    in_specs, args = [], []
    for arr, width, blk in data:
        in_specs.append(pl.BlockSpec((None, CHUNK, width), lambda b, d, n, blk=blk: (b, cidx(n, d), blk)))
        args.append(arr)
    for arr in dir_params:
        in_specs.append(pl.BlockSpec((None,) + arr.shape[1:], lambda b, d, n: (d, 0, 0)))
        args.append(arr)
    for arr in shared_params:
        in_specs.append(pl.BlockSpec(arr.shape, lambda b, d, n: (0, 0)))
        args.append(arr)
    st_shape = init.shape[2:]
    st_spec = pl.BlockSpec((None, None) + st_shape, lambda b, d, n: (b, d, 0, 0, 0))
    in_specs.append(st_spec)
    args.append(init)
    return pl.pallas_call(
        kern,
        grid=(bsz, 2, n_chunks),
        in_specs=in_specs,
        out_specs=[pl.BlockSpec((None, None, CHUNK, GROUP), lambda b, d, n: (d, b, cidx(n, d), 0)),
                   st_spec],
        out_shape=[jax.ShapeDtypeStruct((2, bsz, seq, GROUP), F32),
                   jax.ShapeDtypeStruct(init.shape, F32)],
        scratch_shapes=[pltpu.VMEM(st_shape, F32)],
        compiler_params=_params("parallel", "parallel", "arbitrary"),
        name=name,
    )(*args)


def _expand_matrix(row0):
    d = jnp.arange(2)[:, None, None]
    r = jnp.arange(LANE)[None, :, None]
    c = jnp.arange(GROUP)[None, None, :]
    return (r == row0 + N_HEADS * d + c // HEAD_DIM).astype(BF16)


def _per_head_lanes(t):
    return jnp.repeat(t.astype(F32), HEAD_DIM, axis=-1)[:, None, :]


def _rows_at(w, row0, total):
    pad = [(0, 0)] * (w.ndim - 2) + [(row0, total - row0 - w.shape[-2]), (0, 0)]
    return jnp.pad(w.astype(F32), pad)


def _mixer_params(lp):
    f = lambda t: t.astype(F32).reshape(1, GROUP)
    return dict(
        gla_w2=jnp.stack([_rows_at(lp['gla_gate_w2'][0], 0, LANE), _rows_at(lp['gla_gate_w2'][1], 16, LANE)]),
        gla_b=lp['gla_gate_b'].astype(F32)[:, None, :],
        rwkv_w0=lp['rwkv_w0'].astype(F32)[:, None, :],
        rwkv_ww=jnp.stack([_rows_at(lp['rwkv_w_w2'][0], 0, GROUP), _rows_at(lp['rwkv_w_w2'][1], 32, GROUP)]),
        rwkv_a0=f(lp['rwkv_a0']),
        rwkv_wa=_rows_at(lp['rwkv_a_w2'], 64, GROUP),
        rwkv_wg=_rows_at(lp['rwkv_g_w2'], 96, GROUP),
        rwkv_kk=f(lp['rwkv_k_k']),
        rwkv_ka=f(lp['rwkv_k_a']),
        ssd_e=_expand_matrix(0),
        ssd_dtb=_per_head_lanes(lp['ssm_dt_bias']),
        ssd_alog=_per_head_lanes(lp['ssm_A_log']),
        gdn_eb=_expand_matrix(0),
        gdn_ea=_expand_matrix(8),
        gdn_dtb=_per_head_lanes(lp['dn_dt_bias']),
        gdn_alog=_per_head_lanes(lp['dn_A_log']),
    )


def _token_mixers(proj, convs, mp, init):
    pa, pb, pc, pd = proj
    (xs, bm, cm), (dq, dk, dv) = convs
    tail_blk = 4 * GROUP // LANE
    oa, sa = _scan_call(_gla_kernel, "gla_scan", [(pa, 3 * GROUP, 0), (pa, LANE, tail_blk)],
                        [mp['gla_w2'], mp['gla_b']], [], init[0])
    ob, sb = _scan_call(_rwkv_kernel, "rwkv_scan", [(pb, WB, 0)],
                        [mp['rwkv_w0'], mp['rwkv_ww']],
                        [mp['rwkv_a0'], mp['rwkv_wa'], mp['rwkv_kk'], mp['rwkv_ka']], init[1])
    oc, sc = _scan_call(_ssd_kernel, "ssd_scan",
                        [(pc, LANE, tail_blk), (xs, GROUP, 0), (bm, GROUP, 0), (cm, GROUP, 0)],
                        [mp['ssd_e'], mp['ssd_dtb'], mp['ssd_alog']], [], init[2])
    od, sd = _scan_call(_gdn_kernel, "gdn_scan",
                        [(pd, LANE, tail_blk), (dq, GROUP, 0), (dk, GROUP, 0), (dv, GROUP, 0)],
                        [mp['gdn_eb'], mp['gdn_ea'], mp['gdn_dtb'], mp['gdn_alog']], [], init[3])
    return (oa, ob, oc, od), (sa, sb, sc, sd)


V_GLA_G, V_A0, V_KA, V_RK, V_LN_G, V_LN_B, V_SSM_D, V_SSM_G, V_DN_G = range(9)


def _group_ones():
    r = lax.broadcasted_iota(jnp.int32, (GROUP, GROUP), 0) // SSM_STATE
    c = lax.broadcasted_iota(jnp.int32, (GROUP, GROUP), 1) // SSM_STATE
    return jnp.where(r == c, 1.0, 0.0).astype(BF16)


def _outproj_kernel(x_ref, ga_ref, oa_ref, ob_ref, oc_ref, od_ref, ag_ref, pb_ref, cz_ref, xs_ref,
                    dz_ref, vec_ref, wa_ref, wg_ref, wout_ref, o_ref):
    vec = lambda i: vec_ref[i:i + 1, :]
    inv_hd = 1.0 / HEAD_DIM

    o = oa_ref[0] + oa_ref[1]
    ya = o * lax.rsqrt(_head_sum(o * o) * inv_hd + RMS_EPS) * vec(V_GLA_G) * _silu(ag_ref[...])

    pb = pb_ref[...]
    r = pb[:, :GROUP]
    k = pb[:, GROUP:2 * GROUP]
    v = pb[:, 2 * GROUP:3 * GROUP]
    tail = pb[:, 3 * GROUP:]
    a = _sigmoid(vec(V_A0) + _dot(tail, wa_ref[...]))
    kmod = k * (1.0 + (a - 1.0) * vec(V_KA))
    gate = _dot(_sigmoid(tail), wg_ref[...])
    o = ob_ref[0] + ob_ref[1]
    dev = o - _head_sum(o) * inv_hd
    on = dev * lax.rsqrt(_head_sum(dev * dev) * inv_hd + RWKV_GN_EPS) * vec(V_LN_G) + vec(V_LN_B)
    yb = (on + _head_sum(r * kmod * vec(V_RK)) * v) * gate

    y = (oc_ref[0] + oc_ref[1] + vec(V_SSM_D) * xs_ref[...]) * _silu(cz_ref[...])
    yc = y * lax.rsqrt(_dot_sel_r(y * y, _group_ones()) * (1.0 / SSM_STATE) + RMS_EPS) * vec(V_SSM_G)

    o = od_ref[0] + od_ref[1]
    yd = o * lax.rsqrt(_head_sum(o * o) * inv_hd + RMS_EPS) * vec(V_DN_G) * _silu(dz_ref[...])

    cat = jnp.concatenate([ya, yb, yc, yd], axis=1).astype(BF16)
    o_ref[...] = x_ref[...] + ga_ref[...] * _dot(cat, wout_ref[...])


def _out_projection(x, gate, outs, proj, xs, vecs, wa, wg, w_out):
    bsz, seq, d = x.shape
    tm = 256
    pa, pb, pc, pd = proj
    o_spec = pl.BlockSpec((2, None, tm, GROUP), lambda b, i: (0, b, i, 0))
    col = lambda blk: pl.BlockSpec((None, tm, GROUP), lambda b, i: (b, i, blk))
    full = lambda arr: pl.BlockSpec(arr.shape, lambda b, i: (0, 0))
    return pl.pallas_call(
        _outproj_kernel,
        grid=(bsz, seq // tm),
        in_specs=[pl.BlockSpec((None, tm, d), lambda b, i: (b, i, 0)),
                  pl.BlockSpec((None, 1, d), lambda b, i: (b, 0, 0)),
                  o_spec, o_spec, o_spec, o_spec,
                  col(3),
                  pl.BlockSpec((None, tm, WB), lambda b, i: (b, i, 0)),
                  col(0), col(0), col(3),
                  full(vecs), full(wa), full(wg), full(w_out)],
        out_specs=pl.BlockSpec((None, tm, d), lambda b, i: (b, i, 0)),
        out_shape=jax.ShapeDtypeStruct(x.shape, F32),
        compiler_params=_params("parallel", "parallel"),
        name="out_projection",
    )(x, gate, *outs, pa, pb, pc, xs, pd, vecs, wa, wg, w_out)


def _route_kernel(x_ref, g_ref, sh_ref, sc_ref, rw_ref, rb_ref, h_ref, wd_ref):
    h = _modulated_norm(x_ref[...], g_ref[...], sh_ref[...], sc_ref[...])
    h_ref[...] = h.astype(BF16)
    scores = _sigmoid(_dot3(h, rw_ref[...]))
    sel = scores + rb_ref[...]
    lane = lax.broadcasted_iota(jnp.int32, sel.shape, 1)
    chosen = jnp.zeros(sel.shape, F32)
    for _ in range(TOP_K):
        m = jnp.max(sel, axis=-1, keepdims=True)
        first = jnp.min(jnp.where(sel == m, lane, N_EXPERTS), axis=-1, keepdims=True)
        pick = lane == first
        chosen = jnp.where(pick, 1.0, chosen)
        sel = jnp.where(pick, -jnp.inf, sel)
    w = scores * chosen
    wd_ref[...] = w / jnp.sum(w, axis=-1, keepdims=True) * ROUTED_SCALE


def _route(x, g, shift, scale, router_w, router_bias):
    bsz, seq, d = x.shape
    tm = 256
    vec = pl.BlockSpec((None, 1, d), lambda b, i: (b, 0, 0))
    return pl.pallas_call(
        _route_kernel,
        grid=(bsz, seq // tm),
        in_specs=[pl.BlockSpec((None, tm, d), lambda b, i: (b, i, 0)),
                  pl.BlockSpec((1, d), lambda b, i: (0, 0)),
                  vec, vec,
                  pl.BlockSpec((d, N_EXPERTS), lambda b, i: (0, 0)),
                  pl.BlockSpec((1, N_EXPERTS), lambda b, i: (0, 0))],
        out_specs=[pl.BlockSpec((None, tm, d), lambda b, i: (b, i, 0)),
                   pl.BlockSpec((None, tm, N_EXPERTS), lambda b, i: (b, i, 0))],
        out_shape=[jax.ShapeDtypeStruct((bsz, seq, d), BF16),
                   jax.ShapeDtypeStruct((bsz, seq, N_EXPERTS), F32)],
        compiler_params=_params("parallel", "parallel"),
        name="moe_route",
    )(x, g.reshape(1, d), shift, scale, router_w, router_bias.reshape(1, N_EXPERTS))


def _swiglu_hidden(h, w_gate_up):
    gu = _dot(h, w_gate_up)
    return _silu(gu[:, :EXPERT_DIM]) * gu[:, EXPERT_DIM:]


def _experts_kernel(h_ref, wd_ref, x_ref, gf_ref, wgu_ref, wdn_ref, sgu_ref, sdn_ref, fg_ref,
                    o_ref, acc_ref, *, final_norm):
    e = pl.program_id(2)
    h = h_ref[...]

    @pl.when(e == 0)
    def _():
        acc_ref[...] = _dot(_swiglu_hidden(h, sgu_ref[...]).astype(BF16), sdn_ref[...])

    onehot = lax.broadcasted_iota(jnp.int32, (N_EXPERTS, EXPERT_DIM), 0) == e
    w_e = _dot_sel_r(wd_ref[...], jnp.where(onehot, 1.0, 0.0).astype(BF16))
    act = _swiglu_hidden(h, wgu_ref[...]) * w_e
    acc_ref[...] += _dot(act.astype(BF16), wdn_ref[...])

    @pl.when(e == N_EXPERTS - 1)
    def _():
        y = x_ref[...] + gf_ref[...] * acc_ref[...]
        if final_norm:
            y = y * lax.rsqrt(jnp.mean(y * y, axis=-1, keepdims=True) + RMS_EPS) * fg_ref[...]
        o_ref[...] = y


def _experts(h, wd, x, gate, w_gate_up, w_down, sh_gate_up, sh_down, final_g, final_norm):
    bsz, seq, d = x.shape
    tm = min(seq, 1024)
    tok = lambda w: pl.BlockSpec((None, tm, w), lambda b, i, e: (b, i, 0))
    return pl.pallas_call(
        functools.partial(_experts_kernel, final_norm=final_norm),
        grid=(bsz, seq // tm, N_EXPERTS),
        in_specs=[tok(d), tok(N_EXPERTS), tok(d),
                  pl.BlockSpec((None, 1, d), lambda b, i, e: (b, 0, 0)),
                  pl.BlockSpec((None, d, 2 * EXPERT_DIM), lambda b, i, e: (e, 0, 0)),
                  pl.BlockSpec((None, EXPERT_DIM, d), lambda b, i, e: (e, 0, 0)),
                  pl.BlockSpec((d, 2 * EXPERT_DIM), lambda b, i, e: (0, 0)),
                  pl.BlockSpec((EXPERT_DIM, d), lambda b, i, e: (0, 0)),
                  pl.BlockSpec((1, d), lambda b, i, e: (0, 0))],
        out_specs=tok(d),
        out_shape=jax.ShapeDtypeStruct(x.shape, F32),
        scratch_shapes=[pltpu.VMEM((tm, d), F32)],
        compiler_params=_params("parallel", "parallel", "arbitrary"),
        name="moe_experts",
    )(h, wd, x, gate, w_gate_up, w_down, sh_gate_up, sh_down, final_g.reshape(1, d))


def _pad_w_in(w):
    cuts = (0, 1056, 1984, 3016, 4056)
    widths = (WA, WB, WC, WD)
    parts = []
    for lo, hi, wd in zip(cuts[:-1], cuts[1:], widths):
        parts.append(jnp.pad(w[:, lo:hi], ((0, 0), (0, wd - (hi - lo)))))
    return jnp.concatenate(parts, axis=1).astype(BF16)


def kernel(x, c, ctx, c_ctx, norm1_g, norm2_g, w_mod, b_mod, w_in, w_out, gla_gate_w2, gla_gate_b, gla_norm_g, rwkv_w_w2, rwkv_w0, rwkv_a_w2, rwkv_a0, rwkv_g_w2, rwkv_k_k, rwkv_k_a, rwkv_r_k, rwkv_ln_g, rwkv_ln_b, ssm_conv_w, ssm_conv_b, ssm_A_log, ssm_dt_bias, ssm_D, ssm_norm_g, dn_conv_w, dn_A_log, dn_dt_bias, dn_norm_g, router_w, router_bias, exp_w_gate, exp_w_up, exp_w_down, sh_w_gate, sh_w_up, sh_w_down, final_norm_g):
    bsz, n_lat, d = x.shape
    n_ctx = ctx.shape[1]
    depth = w_in.shape[0]
    rows = n_lat // GRID_W
    x = x.astype(F32)
    ctx = ctx.astype(F32)
    c_rows = jnp.concatenate([c, c_ctx[None, :], jnp.zeros((8 - bsz - 1, d), c.dtype)], axis=0).astype(F32)
    zero_init = tuple(jnp.zeros((bsz, 2, N_HEADS, HEAD_DIM, w), F32)
                      for w in (HEAD_DIM, HEAD_DIM, SSM_STATE, HEAD_DIM))
    for i in range(depth):
        last = i == depth - 1
        lp = dict(gla_gate_w2=gla_gate_w2[i], gla_gate_b=gla_gate_b[i], rwkv_w_w2=rwkv_w_w2[i],
                  rwkv_w0=rwkv_w0[i], rwkv_a_w2=rwkv_a_w2[i], rwkv_a0=rwkv_a0[i],
                  rwkv_g_w2=rwkv_g_w2[i], rwkv_k_k=rwkv_k_k[i], rwkv_k_a=rwkv_k_a[i],
                  ssm_A_log=ssm_A_log[i], ssm_dt_bias=ssm_dt_bias[i],
                  dn_A_log=dn_A_log[i], dn_dt_bias=dn_dt_bias[i])
        mp = _mixer_params(lp)
        vecs = jnp.stack([gla_norm_g[i], rwkv_a0[i], rwkv_k_a[i], rwkv_r_k[i].reshape(GROUP),
                          rwkv_ln_g[i], rwkv_ln_b[i], jnp.repeat(ssm_D[i], HEAD_DIM), ssm_norm_g[i],
                          dn_norm_g[i]] + [jnp.zeros((GROUP,), F32)] * 7).astype(F32)
        w_in_pad = _pad_w_in(w_in[i])
        w_out_b = w_out[i].astype(BF16)
        w_gate_up = jnp.concatenate([exp_w_gate[i], exp_w_up[i]], axis=-1).astype(BF16)
        w_down = exp_w_down[i].astype(BF16)
        sh_gate_up = jnp.concatenate([sh_w_gate[i], sh_w_up[i]], axis=-1).astype(BF16)
        sh_down = sh_w_down[i].astype(BF16)
        dn_conv_b = jnp.zeros((3 * GROUP,), F32)

        mod = _modulation(c_rows, w_mod[i].astype(F32), b_mod[i].astype(F32))
        mod_x = [m[:, None, :] for m in jnp.split(mod[:bsz], 6, axis=-1)]
        mod_c = [jnp.broadcast_to(m[None], (bsz, 1, d)) for m in jnp.split(mod[bsz:bsz + 1], 6, axis=-1)]

        def mix(tokens, mods, init, g_rows, g_cols):
            proj = _in_projection(tokens, norm1_g[i], mods[0], mods[1], w_in_pad)
            convs = (_grid_conv(proj[2], GROUP, ssm_conv_w[i], ssm_conv_b[i], g_rows, g_cols),
                     _grid_conv(proj[3], 0, dn_conv_w[i], dn_conv_b, g_rows, g_cols))
            outs, states = _token_mixers(proj, convs, mp, init)
            return proj, convs, outs, states

        proj_c, convs_c, outs_c, ctx_states = mix(ctx, mod_c, zero_init, 1, n_ctx)
        proj_x, convs_x, outs_x, _ = mix(x, mod_x, ctx_states, rows, GRID_W)
        x = _out_projection(x, mod_x[2], outs_x, proj_x, convs_x[0][0], vecs,
                            mp['rwkv_wa'], mp['rwkv_wg'], w_out_b)

        def ffn(tokens, mods, final):
            h, wd = _route(tokens, norm2_g[i], mods[3], mods[4], router_w[i].astype(F32),
                           router_bias[i].astype(F32))
            return _experts(h, wd, tokens, mods[5], w_gate_up, w_down, sh_gate_up, sh_down,
                            final_norm_g.astype(F32), final)

        x = ffn(x, mod_x, last)
        if not last:
            ctx = _out_projection(ctx, mod_c[2], outs_c, proj_c, convs_c[0][0], vecs,
                                  mp['rwkv_wa'], mp['rwkv_wg'], w_out_b)
            ctx = ffn(ctx, mod_c, False)
    return x
```

```python
import functools

import jax
import jax.numpy as jnp
from jax import lax
from jax.experimental import pallas as pl
from jax.experimental.pallas import tpu as pltpu

F32 = jnp.float32
BF16 = jnp.bfloat16

D_MODEL = 1024
GROUP = 256
N_HEADS = 4
HEAD_DIM = 64
CHUNK = 64
GRID_W = 64
SSM_STATE = 128
N_EXPERTS = 64
TOP_K = 8
EXPERT_DIM = 256
ROUTED_SCALE = 2.5
RMS_EPS = 1e-6
RWKV_GN_EPS = 64e-5
GLA_GATE_NORM = 16.0
LANE = 128
CONV_HALO = 128
VMEM_LIMIT = 56 * 1024 * 1024

WA = 4 * GROUP + LANE
WB = 4 * GROUP
WC = 4 * GROUP + LANE
WD = 4 * GROUP + LANE
W_IN_PAD = WA + WB + WC + WD

NT = (((1,), (1,)), ((), ()))
TN = (((0,), (0,)), ((), ()))


def _dot(a, b):
    return jnp.dot(a, b, preferred_element_type=F32)


def _dot_nt(a, b):
    return lax.dot_general(a, b, NT, preferred_element_type=F32)


def _dot_tn(a, b):
    return lax.dot_general(a, b, TN, preferred_element_type=F32)


def _split3(x):
    hi = x.astype(BF16)
    r1 = x - hi.astype(F32)
    mid = r1.astype(BF16)
    lo = (r1 - mid.astype(F32)).astype(BF16)
    return hi, mid, lo


def _dot_sel_l(sel, x):
    hi, mid, lo = _split3(x)
    return _dot(sel, hi) + _dot(sel, mid) + _dot(sel, lo)


def _dot_sel_r(x, sel):
    hi, mid, lo = _split3(x)
    return _dot(hi, sel) + _dot(mid, sel) + _dot(lo, sel)


def _dot3(a, b):
    ah, am, _ = _split3(a)
    bh, bm, _ = _split3(b)
    return _dot(ah, bh) + (_dot(ah, bm) + _dot(am, bh))


def _softplus(x):
    return jnp.maximum(x, 0.0) + jnp.log1p(jnp.exp(-jnp.abs(x)))


def _sigmoid(x):
    return 1.0 / (1.0 + jnp.exp(-x))


def _silu(x):
    return x * _sigmoid(x)


def _head_ones():
    r = lax.broadcasted_iota(jnp.int32, (GROUP, GROUP), 0) // HEAD_DIM
    c = lax.broadcasted_iota(jnp.int32, (GROUP, GROUP), 1) // HEAD_DIM
    return jnp.where(r == c, 1.0, 0.0).astype(BF16)


def _head_sum(x):
    return _dot_sel_r(x, _head_ones())


def _params(*sem):
    return pltpu.CompilerParams(dimension_semantics=sem, vmem_limit_bytes=VMEM_LIMIT)


def _mod_kernel(c_ref, w_ref, b_ref, o_ref):
    o_ref[...] = _dot3(_silu(c_ref[...]), w_ref[...]) + b_ref[...]


def _modulation(c_rows, w, b):
    m, d = c_rows.shape
    n = w.shape[1]
    tn = 1536
    return pl.pallas_call(
        _mod_kernel,
        grid=(n // tn,),
        in_specs=[pl.BlockSpec((m, d), lambda j: (0, 0)),
                  pl.BlockSpec((d, tn), lambda j: (0, j)),
                  pl.BlockSpec((1, tn), lambda j: (0, j))],
        out_specs=pl.BlockSpec((m, tn), lambda j: (0, j)),
        out_shape=jax.ShapeDtypeStruct((m, n), F32),
        compiler_params=_params("parallel"),
        name="modulation",
    )(c_rows, w, b.reshape(1, n))


def _modulated_norm(x, g, shift, scale):
    y = x * lax.rsqrt(jnp.mean(x * x, axis=-1, keepdims=True) + RMS_EPS)
    return y * g * (1.0 + scale) + shift


def _inproj_kernel(x_ref, g_ref, sh_ref, sc_ref, w_ref, oa_ref, ob_ref, oc_ref, od_ref):
    h = _modulated_norm(x_ref[...], g_ref[...], sh_ref[...], sc_ref[...]).astype(BF16)
    y = _dot(h, w_ref[...])
    oa_ref[...] = y[:, :WA]
    ob_ref[...] = y[:, WA:WA + WB]
    oc_ref[...] = y[:, WA + WB:WA + WB + WC]
    od_ref[...] = y[:, WA + WB + WC:]


def _in_projection(x, g, shift, scale, w_pad):
    bsz, seq, d = x.shape
    tm = 256
    vec = pl.BlockSpec((None, 1, d), lambda b, i: (b, 0, 0))
    out = lambda w: pl.BlockSpec((None, tm, w), lambda b, i: (b, i, 0))
    shp = lambda w: jax.ShapeDtypeStruct((bsz, seq, w), F32)
    return pl.pallas_call(
        _inproj_kernel,
        grid=(bsz, seq // tm),
        in_specs=[pl.BlockSpec((None, tm, d), lambda b, i: (b, i, 0)),
                  pl.BlockSpec((1, d), lambda b, i: (0, 0)),
                  vec, vec,
                  pl.BlockSpec((d, W_IN_PAD), lambda b, i: (0, 0))],
        out_specs=[out(WA), out(WB), out(WC), out(WD)],
        out_shape=[shp(WA), shp(WB), shp(WC), shp(WD)],
        compiler_params=_params("parallel", "parallel"),
        name="in_projection",
    )(x, g.reshape(1, d), shift, scale, w_pad)


def _conv_kernel(prev_ref, cur_ref, next_ref, w_ref, b_ref, o_ref, ext_ref, *, cols, taps_r, n_tiles):
    i = pl.program_id(1)
    tm, ch = cur_ref.shape
    ext_ref[CONV_HALO:CONV_HALO + tm, :] = cur_ref[...]
    ext_ref[:CONV_HALO, :] = jnp.where(i > 0, prev_ref[...], 0.0)
    ext_ref[CONV_HALO + tm:, :] = jnp.where(i < n_tiles - 1, next_ref[...], 0.0)
    rt = 128
    for r0 in range(0, tm, rt):
        col = (lax.broadcasted_iota(jnp.int32, (rt, LANE), 0) + r0) % cols
        for c0 in range(0, ch, LANE):
            acc = jnp.zeros((rt, LANE), F32) + b_ref[:, c0:c0 + LANE]
            for dr in taps_r:
                for dc in (-1, 0, 1):
                    off = CONV_HALO + r0 + dr * cols + dc
                    tap = ext_ref[off:off + rt, c0:c0 + LANE]
                    if dc != 0:
                        tap = jnp.where((col + dc >= 0) & (col + dc < cols), tap, 0.0)
                    k = (dr + 1) * 3 + (dc + 1)
                    acc = acc + tap * w_ref[k:k + 1, c0:c0 + LANE]
            o_ref[r0:r0 + rt, c0:c0 + LANE] = _silu(acc)


def _grid_conv(proj, lane0, w, b, rows, cols):
    bsz, seq, _ = proj.shape
    tm = min(seq, 512)
    n_tiles = seq // tm
    hpt = tm // CONV_HALO
    n_halo = seq // CONV_HALO
    reach = cols + 1 if rows > 1 else 1
    assert seq == rows * cols and tm % cols == 0 and reach <= CONV_HALO and lane0 % GROUP == 0
    kern = functools.partial(_conv_kernel, cols=cols, taps_r=(-1, 0, 1) if rows > 1 else (0,),
                             n_tiles=n_tiles)
    w9 = w.reshape(9, 3 * GROUP)
    b1 = b.reshape(1, 3 * GROUP)
    outs = []
    for j in range(3):
        lb = lane0 // GROUP + j
        outs.append(pl.pallas_call(
            kern,
            grid=(bsz, n_tiles),
            in_specs=[
                pl.BlockSpec((None, CONV_HALO, GROUP),
                             lambda bi, i, lb=lb: (bi, jnp.maximum(i * hpt - 1, 0), lb)),
                pl.BlockSpec((None, tm, GROUP), lambda bi, i, lb=lb: (bi, i, lb)),
                pl.BlockSpec((None, CONV_HALO, GROUP),
                             lambda bi, i, lb=lb: (bi, jnp.minimum((i + 1) * hpt, n_halo - 1), lb)),
                pl.BlockSpec((9, GROUP), lambda bi, i, j=j: (0, j)),
                pl.BlockSpec((1, GROUP), lambda bi, i, j=j: (0, j))],
            out_specs=pl.BlockSpec((None, tm, GROUP), lambda bi, i: (bi, i, 0)),
            out_shape=jax.ShapeDtypeStruct((bsz, seq, GROUP), F32),
            scratch_shapes=[pltpu.VMEM((tm + 2 * CONV_HALO, GROUP), F32)],
            compiler_params=_params("parallel", "parallel"),
            name="grid_conv",
        )(proj, proj, proj, w9, b1))
    return outs


HW = N_HEADS * CHUNK


def _scan_prologue(s0_ref, s_scr):
    @pl.when(pl.program_id(1) == 0)
    def _():
        s_scr[...] = s0_ref[...]


def _scan_epilogue(sfin_ref, s_scr):
    @pl.when(pl.program_id(1) == pl.num_programs(1) - 1)
    def _():
        sfin_ref[...] = s_scr[...]


def _token_masks(direction, width):
    r = lax.broadcasted_iota(jnp.int32, (CHUNK, width), 0)
    c = lax.broadcasted_iota(jnp.int32, (CHUNK, width), 1) % CHUNK
    diff = (r - c) * (1 - 2 * direction)
    return diff >= 0, diff > 0


def _stacked_masks(direction):
    r = lax.broadcasted_iota(jnp.int32, (HW, HW), 0)
    c = lax.broadcasted_iota(jnp.int32, (HW, HW), 1)
    same = (r // CHUNK) == (c // CHUNK)
    diff = (r - c) * (1 - 2 * direction)
    return same, same & (diff >= 0), same & (diff > 0), r == c


def _tile4(x):
    return jnp.concatenate([x] * N_HEADS, axis=0)


def _stack4(x, same):
    return jnp.where(same, _tile4(x), 0.0)


def _unstack4(y):
    return (y[:CHUNK] + y[CHUNK:2 * CHUNK]) + (y[2 * CHUNK:3 * CHUNK] + y[3 * CHUNK:])


def _neumann_inverse_many(ns, eye):
    ts = [eye + n for n in ns]
    ps = [n.astype(BF16) for n in ns]
    for _ in range(5):
        ps = [_dot(p, p).astype(BF16) for p in ps]
        ts = [t + _dot(t.astype(BF16), p) for t, p in zip(ts, ps)]
    return ts


def _cum_rows_cols(g, incl_f, incl_t_f):
    cc = _dot_sel_l(incl_f, g)
    ones = jnp.ones((CHUNK, CHUNK), BF16)
    rr = _dot_sel_l(ones, g * jnp.concatenate([incl_t_f] * N_HEADS, axis=1))
    return cc, rr


def _gla_kernel(qkv_f, tail_f, qkv_b, tail_b, w2_ref, b_ref, s0_ref, of_ref, ob_ref, sfin_ref, s_scr):
    _scan_prologue(s0_ref, s_scr)
    for d, (qkv_ref, tail_ref, o_ref) in enumerate(((qkv_f, tail_f, of_ref), (qkv_b, tail_b, ob_ref))):
        incl, _ = _token_masks(d, CHUNK)
        p = qkv_ref[...]
        q = p[:, :GROUP] * HEAD_DIM ** -0.5
        k = p[:, GROUP:2 * GROUP]
        v = p[:, 2 * GROUP:]
        zg = _dot(tail_ref[...], w2_ref[d]) + b_ref[d]
        gk = (jnp.minimum(zg, 0.0) - jnp.log1p(jnp.exp(-jnp.abs(zg)))) / GLA_GATE_NORM
        bc = _dot_sel_l(jnp.where(incl, 1.0, 0.0).astype(BF16), gk)
        bt = jnp.sum(gk, axis=0, keepdims=True)
        qt = q * jnp.exp(bc)
        kt = k * jnp.exp(-bc)
        ks = k * jnp.exp(bt - bc)
        dec = jnp.exp(bt)
        for h in range(N_HEADS):
            sl = slice(h * HEAD_DIM, (h + 1) * HEAD_DIM)
            st = s_scr[d, h]
            att = jnp.where(incl, _dot_nt(qt[:, sl], kt[:, sl]), 0.0)
            o_ref[:, sl] = _dot(att, v[:, sl]) + _dot_nt(qt[:, sl], st)
            s_scr[d, h] = st * dec[:, sl] + _dot_tn(v[:, sl], ks[:, sl])
    _scan_epilogue(sfin_ref, s_scr)


def _ssd_kernel(tail_f, xs_f, bm_f, cm_f, tail_b, xs_b, bm_b, cm_b, e_ref, dtb_ref, alog_ref, s0_ref,
                of_ref, ob_ref, sfin_ref, s_scr):
    _scan_prologue(s0_ref, s_scr)
    for d, (tail_ref, xs_ref, bm_ref, cm_ref, o_ref) in enumerate(
            ((tail_f, xs_f, bm_f, cm_f, of_ref), (tail_b, xs_b, bm_b, cm_b, ob_ref))):
        incl, _ = _token_masks(d, CHUNK)
        incl_f = jnp.where(incl, 1.0, 0.0)
        dt = _softplus(_dot_sel_r(tail_ref[...], e_ref[d]) + dtb_ref[d])
        a = -jnp.exp(alog_ref[d]) * dt
        x = xs_ref[...] * dt
        cc, rr = _cum_rows_cols(a, incl_f.astype(BF16), incl_f.T)
        at = jnp.sum(a, axis=0, keepdims=True)
        e_cc = jnp.exp(cc)
        xd = x * jnp.exp(at - cc)
        dec = jnp.exp(at)
        bm = bm_ref[...]
        cm = cm_ref[...]
        for g in range(2):
            gs = slice(g * SSM_STATE, (g + 1) * SSM_STATE)
            cb = _dot_nt(cm[:, gs], bm[:, gs])
            for h in range(2 * g, 2 * g + 2):
                sl = slice(h * HEAD_DIM, (h + 1) * HEAD_DIM)
                s = s_scr[d, h]
                lm = jnp.exp(jnp.where(incl, cc[:, sl] - rr[:, sl], -jnp.inf))
                o_ref[:, sl] = _dot(cb * lm, x[:, sl]) + _dot_nt(cm[:, gs], s) * e_cc[:, sl]
                s_scr[d, h] = s * dec[:, h * HEAD_DIM:h * HEAD_DIM + 1] + _dot_tn(xd[:, sl], bm[:, gs])
    _scan_epilogue(sfin_ref, s_scr)


def _rwkv_kernel(p_f, p_b, w0_ref, ww_ref, a0_ref, wa_ref, kk_ref, ka_ref, s0_ref,
                 of_ref, ob_ref, sfin_ref, s_scr):
    _scan_prologue(s0_ref, s_scr)
    dirs = (0, 1)
    prep = []
    for d, p_ref in zip(dirs, (p_f, p_b)):
        same, incl4, strict4, diag4 = _stacked_masks(d)
        incl_t, _ = _token_masks(d, CHUNK)
        p = p_ref[...]
        r = p[:, :GROUP]
        k = p[:, GROUP:2 * GROUP]
        v = p[:, 2 * GROUP:3 * GROUP]
        tail = p[:, 3 * GROUP:]
        wr = w0_ref[d] + _dot(jnp.tanh(tail), ww_ref[d])
        lw = -jnp.exp(-_softplus(-wr) - 0.5)
        a = _sigmoid(a0_ref[...] + _dot(tail, wa_ref[...]))
        kk = k * kk_ref[...]
        kk = kk * lax.rsqrt(_head_sum(kk * kk) + 1e-6)
        kmod = k * (1.0 + (a - 1.0) * ka_ref[...])
        bv = kk * a
        pc = _dot_sel_l(jnp.where(incl_t, 1.0, 0.0).astype(BF16), lw)
        pt = jnp.sum(lw, axis=0, keepdims=True)
        e_npc = jnp.exp(-pc)
        e_rest = jnp.exp(pt - pc)
        zr = jnp.concatenate([_stack4(-kk * jnp.exp(pc - lw), same), _stack4(r * jnp.exp(pc), same)],
                             axis=0).astype(BF16)
        prep.append(dict(
            same=same, incl4=incl4, strict4=strict4, zr=zr,
            bt=_tile4(bv * e_npc).astype(BF16), kt=_tile4(kmod * e_npc).astype(BF16),
            v4=_stack4(v, same).astype(BF16),
            uvb=jnp.concatenate([_tile4(bv * e_rest), _tile4(kmod * e_rest)], axis=0).astype(BF16),
            dec=jnp.exp(pt), eye=jnp.where(diag4, 1.0, 0.0)))
    ab = [_dot_nt(c['zr'], c['bt']) for c in prep]
    ak = [_dot_nt(c['zr'], c['kt']) for c in prep]
    a_zb = [jnp.where(c['strict4'], m[:HW], 0.0) for c, m in zip(prep, ab)]
    a_zk = [jnp.where(c['strict4'], m[:HW], 0.0).astype(BF16) for c, m in zip(prep, ak)]
    a_rb = [jnp.where(c['incl4'], m[HW:], 0.0).astype(BF16) for c, m in zip(prep, ab)]
    a_rk = [jnp.where(c['incl4'], m[HW:], 0.0).astype(BF16) for c, m in zip(prep, ak)]
    ts = [t.astype(BF16) for t in _neumann_inverse_many(a_zb, prep[0]['eye'])]
    sts = [s_scr[d] for d in dirs]
    stb = [s.astype(BF16) for s in sts]
    wu = [_dot(t, c['zr'][:HW]) for t, c in zip(ts, prep)]
    zkv = [_dot(m, c['v4']) for m, c in zip(a_zk, prep)]
    uv = [_dot(t, m.astype(BF16)) for t, m in zip(ts, zkv)]
    u = [_dot_nt(w.astype(BF16), s) + x for w, s, x in zip(wu, stb, uv)]
    ub = [x.astype(BF16) for x in u]
    y = [_dot_nt(c['zr'][HW:], s) + _dot(m1, x) + _dot(m2, c['v4'])
         for c, s, m1, x, m2 in zip(prep, stb, a_rb, ub, a_rk)]
    for d, o_ref in zip(dirs, (of_ref, ob_ref)):
        c = prep[d]
        o_ref[...] = _unstack4(y[d])
        upd = _dot_tn(jnp.concatenate([ub[d], c['v4']], axis=0), c['uvb'])
        s_scr[d] = sts[d] * c['dec'] + jnp.where(c['same'], upd, 0.0)
    _scan_epilogue(sfin_ref, s_scr)


def _gdn_kernel(tail_f, q_f, k_f, v_f, tail_b, q_b, k_b, v_b, eb_ref, ea_ref, dtb_ref, alog_ref, s0_ref,
                of_ref, ob_ref, sfin_ref, s_scr):
    _scan_prologue(s0_ref, s_scr)
    dirs = (0, 1)
    prep = []
    for d, (tail_ref, q_ref, k_ref, v_ref) in zip(dirs, ((tail_f, q_f, k_f, v_f), (tail_b, q_b, k_b, v_b))):
        same, _, _, diag4 = _stacked_masks(d)
        incl_t, strict_t = _token_masks(d, GROUP)
        incl_c, _ = _token_masks(d, CHUNK)
        incl_f = jnp.where(incl_c, 1.0, 0.0)
        q = q_ref[...]
        q = q * lax.rsqrt(_head_sum(q * q) + 1e-6) * HEAD_DIM ** -0.5
        k = k_ref[...]
        k = k * lax.rsqrt(_head_sum(k * k) + 1e-6)
        v = v_ref[...]
        tail = tail_ref[...]
        beta = _sigmoid(_dot_sel_r(tail, eb_ref[d]))
        g = -jnp.exp(alog_ref[d]) * _softplus(_dot_sel_r(tail, ea_ref[d]) + dtb_ref[d])
        cc, rr = _cum_rows_cols(g, incl_f.astype(BF16), incl_f.T)
        seg = cc - rr
        gt = jnp.sum(g, axis=0, keepdims=True)
        e_cc = jnp.exp(cc)
        kb = k * beta
        prep.append(dict(
            same=same,
            d_strict=_stack4(jnp.exp(jnp.where(strict_t, seg, -jnp.inf)), same),
            d_incl=_stack4(jnp.exp(jnp.where(incl_t, seg, -jnp.inf)), same),
            kb4=_stack4(kb, same).astype(BF16), kt=_tile4(k).astype(BF16),
            q4=_stack4(q, same).astype(BF16),
            rhs=jnp.concatenate([_stack4(kb * e_cc, same), _stack4(v * beta, same)], axis=1).astype(BF16),
            qd4=_stack4(q * e_cc, same).astype(BF16),
            kd4=_stack4(k * jnp.exp(gt - cc), same).astype(BF16),
            dec=jnp.exp(gt), eye=jnp.where(diag4, 1.0, 0.0)))
    a_kk = [_dot_nt(c['kb4'], c['kt']) * c['d_strict'] for c in prep]
    a_qk = [(_dot_nt(c['q4'], c['kt']) * c['d_incl']).astype(BF16) for c in prep]
    ts = [t.astype(BF16) for t in _neumann_inverse_many([-a for a in a_kk], prep[0]['eye'])]
    sol = [_dot(t, c['rhs']) for t, c in zip(ts, prep)]
    sts = [s_scr[d] for d in dirs]
    stb = [s.astype(BF16) for s in sts]
    v_new = [(x[:, HW:] - _dot(x[:, :HW].astype(BF16), s)).astype(BF16) for x, s in zip(sol, stb)]
    o = [_dot(c['qd4'], s) + _dot(m, vn) for c, s, m, vn in zip(prep, stb, a_qk, v_new)]
    for d, o_ref in zip(dirs, (of_ref, ob_ref)):
        c = prep[d]
        o_ref[...] = _unstack4(o[d])
        s_scr[d] = sts[d] * c['dec'] + _dot_tn(c['kd4'], v_new[d])
    _scan_epilogue(sfin_ref, s_scr)


def _scan_call(kern, name, data, params, init):
    bsz, seq = data[0][0].shape[:2]
    n_chunks = seq // CHUNK
    in_specs, args = [], []
    for mirror in (False, True):
        for arr, width, blk in data:
            idx = (lambda b, n, blk=blk: (b, n_chunks - 1 - n, blk)) if mirror else \
                  (lambda b, n, blk=blk: (b, n, blk))
            in_specs.append(pl.BlockSpec((None, CHUNK, width), idx))
            args.append(arr)
    for arr in params:
        in_specs.append(pl.BlockSpec(arr.shape, lambda b, n, nd=arr.ndim: (0,) * nd))
        args.append(arr)
    st_shape = init.shape[1:]
    st_spec = pl.BlockSpec((None,) + st_shape, lambda b, n, nd=len(st_shape): (b,) + (0,) * nd)
    in_specs.append(st_spec)
    args.append(init)
    o_shape = jax.ShapeDtypeStruct((bsz, seq, GROUP), F32)
    return pl.pallas_call(
        kern,
        grid=(bsz, n_chunks),
        in_specs=in_specs,
        out_specs=[pl.BlockSpec((None, CHUNK, GROUP), lambda b, n: (b, n, 0)),
                   pl.BlockSpec((None, CHUNK, GROUP), lambda b, n: (b, n_chunks - 1 - n, 0)),
                   st_spec],
        out_shape=[o_shape, o_shape, jax.ShapeDtypeStruct(init.shape, F32)],
        scratch_shapes=[pltpu.VMEM(st_shape, F32)],
        compiler_params=_params("parallel", "arbitrary"),
        name=name,
    )(*args)


def _expand_matrix(row0):
    d = jnp.arange(2)[:, None, None]
    r = jnp.arange(LANE)[None, :, None]
    c = jnp.arange(GROUP)[None, None, :]
    return (r == row0 + N_HEADS * d + c // HEAD_DIM).astype(BF16)


def _per_head_lanes(t):
    return jnp.repeat(t.astype(F32), HEAD_DIM, axis=-1)[:, None, :]


def _rows_at(w, row0, total):
    pad = [(0, 0)] * (w.ndim - 2) + [(row0, total - row0 - w.shape[-2]), (0, 0)]
    return jnp.pad(w.astype(F32), pad)


def _mixer_params(lp):
    f = lambda t: t.astype(F32).reshape(1, GROUP)
    return dict(
        gla_w2=jnp.stack([_rows_at(lp['gla_gate_w2'][0], 0, LANE), _rows_at(lp['gla_gate_w2'][1], 16, LANE)]),
        gla_b=lp['gla_gate_b'].astype(F32)[:, None, :],
        rwkv_w0=lp['rwkv_w0'].astype(F32)[:, None, :],
        rwkv_ww=jnp.stack([_rows_at(lp['rwkv_w_w2'][0], 0, GROUP), _rows_at(lp['rwkv_w_w2'][1], 32, GROUP)]),
        rwkv_a0=f(lp['rwkv_a0']),
        rwkv_wa=_rows_at(lp['rwkv_a_w2'], 64, GROUP),
        rwkv_wg=_rows_at(lp['rwkv_g_w2'], 96, GROUP),
        rwkv_kk=f(lp['rwkv_k_k']),
        rwkv_ka=f(lp['rwkv_k_a']),
        ssd_e=_expand_matrix(0),
        ssd_dtb=_per_head_lanes(lp['ssm_dt_bias']),
        ssd_alog=_per_head_lanes(lp['ssm_A_log']),
        gdn_eb=_expand_matrix(0),
        gdn_ea=_expand_matrix(8),
        gdn_dtb=_per_head_lanes(lp['dn_dt_bias']),
        gdn_alog=_per_head_lanes(lp['dn_A_log']),
    )


def _token_mixers(proj, convs, mp, init):
    pa, pb, pc, pd = proj
    (xs, bm, cm), (dq, dk, dv) = convs
    tail_blk = 4 * GROUP // LANE
    ga = _scan_call(_gla_kernel, "gla_scan", [(pa, 3 * GROUP, 0), (pa, LANE, tail_blk)],
                    [mp['gla_w2'], mp['gla_b']], init[0])
    rb = _scan_call(_rwkv_kernel, "rwkv_scan", [(pb, WB, 0)],
                    [mp['rwkv_w0'], mp['rwkv_ww'], mp['rwkv_a0'], mp['rwkv_wa'], mp['rwkv_kk'],
                     mp['rwkv_ka']], init[1])
    sc = _scan_call(_ssd_kernel, "ssd_scan",
                    [(pc, LANE, tail_blk), (xs, GROUP, 0), (bm, GROUP, 0), (cm, GROUP, 0)],
                    [mp['ssd_e'], mp['ssd_dtb'], mp['ssd_alog']], init[2])
    gd = _scan_call(_gdn_kernel, "gdn_scan",
                    [(pd, LANE, tail_blk), (dq, GROUP, 0), (dk, GROUP, 0), (dv, GROUP, 0)],
                    [mp['gdn_eb'], mp['gdn_ea'], mp['gdn_dtb'], mp['gdn_alog']], init[3])
    outs = tuple(o for call in (ga, rb, sc, gd) for o in call[:2])
    return outs, (ga[2], rb[2], sc[2], gd[2])


V_GLA_G, V_A0, V_KA, V_RK, V_LN_G, V_LN_B, V_SSM_D, V_SSM_G, V_DN_G = range(9)


def _group_ones():
    r = lax.broadcasted_iota(jnp.int32, (GROUP, GROUP), 0) // SSM_STATE
    c = lax.broadcasted_iota(jnp.int32, (GROUP, GROUP), 1) // SSM_STATE
    return jnp.where(r == c, 1.0, 0.0).astype(BF16)


def _outproj_kernel(x_ref, ga_ref, oaf_ref, oab_ref, obf_ref, obb_ref, ocf_ref, ocb_ref, odf_ref, odb_ref,
                    ag_ref, pb_ref, cz_ref, xs_ref, dz_ref, vec_ref, wa_ref, wg_ref, wout_ref, o_ref):
    vec = lambda i: vec_ref[i:i + 1, :]
    inv_hd = 1.0 / HEAD_DIM

    o = oaf_ref[...] + oab_ref[...]
    ya = o * lax.rsqrt(_head_sum(o * o) * inv_hd + RMS_EPS) * vec(V_GLA_G) * _silu(ag_ref[...])

    pb = pb_ref[...]
    r = pb[:, :GROUP]
    k = pb[:, GROUP:2 * GROUP]
    v = pb[:, 2 * GROUP:3 * GROUP]
    tail = pb[:, 3 * GROUP:]
    a = _sigmoid(vec(V_A0) + _dot(tail, wa_ref[...]))
    kmod = k * (1.0 + (a - 1.0) * vec(V_KA))
    gate = _dot(_sigmoid(tail), wg_ref[...])
    o = obf_ref[...] + obb_ref[...]
    dev = o - _head_sum(o) * inv_hd
    on = dev * lax.rsqrt(_head_sum(dev * dev) * inv_hd + RWKV_GN_EPS) * vec(V_LN_G) + vec(V_LN_B)
    yb = (on + _head_sum(r * kmod * vec(V_RK)) * v) * gate

    y = (ocf_ref[...] + ocb_ref[...] + vec(V_SSM_D) * xs_ref[...]) * _silu(cz_ref[...])
    yc = y * lax.rsqrt(_dot_sel_r(y * y, _group_ones()) * (1.0 / SSM_STATE) + RMS_EPS) * vec(V_SSM_G)

    o = odf_ref[...] + odb_ref[...]
    yd = o * lax.rsqrt(_head_sum(o * o) * inv_hd + RMS_EPS) * vec(V_DN_G) * _silu(dz_ref[...])

    cat = jnp.concatenate([ya, yb, yc, yd], axis=1).astype(BF16)
    o_ref[...] = x_ref[...] + ga_ref[...] * _dot(cat, wout_ref[...])


def _out_projection(x, gate, outs, proj, xs, vecs, wa, wg, w_out):
    bsz, seq, d = x.shape
    tm = 256
    pa, pb, pc, pd = proj
    col = lambda blk: pl.BlockSpec((None, tm, GROUP), lambda b, i: (b, i, blk))
    full = lambda arr: pl.BlockSpec(arr.shape, lambda b, i: (0, 0))
    return pl.pallas_call(
        _outproj_kernel,
        grid=(bsz, seq // tm),
        in_specs=[pl.BlockSpec((None, tm, d), lambda b, i: (b, i, 0)),
                  pl.BlockSpec((None, 1, d), lambda b, i: (b, 0, 0)),
                  col(0), col(0), col(0), col(0), col(0), col(0), col(0), col(0),
                  col(3),
                  pl.BlockSpec((None, tm, WB), lambda b, i: (b, i, 0)),
                  col(0), col(0), col(3),
                  full(vecs), full(wa), full(wg), full(w_out)],
        out_specs=pl.BlockSpec((None, tm, d), lambda b, i: (b, i, 0)),
        out_shape=jax.ShapeDtypeStruct(x.shape, F32),
        compiler_params=_params("parallel", "parallel"),
        name="out_projection",
    )(x, gate, *outs, pa, pb, pc, xs, pd, vecs, wa, wg, w_out)


def _route_kernel(x_ref, g_ref, sh_ref, sc_ref, rw_ref, rb_ref, h_ref, wd_ref):
    h = _modulated_norm(x_ref[...], g_ref[...], sh_ref[...], sc_ref[...])
    h_ref[...] = h.astype(BF16)
    scores = _sigmoid(_dot3(h, rw_ref[...]))
    sel = scores + rb_ref[...]
    lane = lax.broadcasted_iota(jnp.int32, sel.shape, 1)
    chosen = jnp.zeros(sel.shape, F32)
    for _ in range(TOP_K):
        m = jnp.max(sel, axis=-1, keepdims=True)
        first = jnp.min(jnp.where(sel == m, lane, N_EXPERTS), axis=-1, keepdims=True)
        pick = lane == first
        chosen = jnp.where(pick, 1.0, chosen)
        sel = jnp.where(pick, -jnp.inf, sel)
    w = scores * chosen
    wd_ref[...] = w / jnp.sum(w, axis=-1, keepdims=True) * ROUTED_SCALE


def _route(x, g, shift, scale, router_w, router_bias):
    bsz, seq, d = x.shape
    tm = 256
    vec = pl.BlockSpec((None, 1, d), lambda b, i: (b, 0, 0))
    return pl.pallas_call(
        _route_kernel,
        grid=(bsz, seq // tm),
        in_specs=[pl.BlockSpec((None, tm, d), lambda b, i: (b, i, 0)),
                  pl.BlockSpec((1, d), lambda b, i: (0, 0)),
                  vec, vec,
                  pl.BlockSpec((d, N_EXPERTS), lambda b, i: (0, 0)),
                  pl.BlockSpec((1, N_EXPERTS), lambda b, i: (0, 0))],
        out_specs=[pl.BlockSpec((None, tm, d), lambda b, i: (b, i, 0)),
                   pl.BlockSpec((None, tm, N_EXPERTS), lambda b, i: (b, i, 0))],
        out_shape=[jax.ShapeDtypeStruct((bsz, seq, d), BF16),
                   jax.ShapeDtypeStruct((bsz, seq, N_EXPERTS), F32)],
        compiler_params=_params("parallel", "parallel"),
        name="moe_route",
    )(x, g.reshape(1, d), shift, scale, router_w, router_bias.reshape(1, N_EXPERTS))


def _swiglu_hidden(h, w_gate_up):
    gu = _dot(h, w_gate_up)
    return _silu(gu[:, :EXPERT_DIM]) * gu[:, EXPERT_DIM:]


def _experts_kernel(h_ref, wd_ref, x_ref, gf_ref, wgu_ref, wdn_ref, sgu_ref, sdn_ref, fg_ref,
                    o_ref, acc_ref, *, final_norm):
    e = pl.program_id(2)
    h = h_ref[...]

    @pl.when(e == 0)
    def _():
        acc_ref[...] = _dot(_swiglu_hidden(h, sgu_ref[...]).astype(BF16), sdn_ref[...])

    onehot = lax.broadcasted_iota(jnp.int32, (N_EXPERTS, EXPERT_DIM), 0) == e
    w_e = _dot_sel_r(wd_ref[...], jnp.where(onehot, 1.0, 0.0).astype(BF16))
    act = _swiglu_hidden(h, wgu_ref[...]) * w_e
    acc_ref[...] += _dot(act.astype(BF16), wdn_ref[...])

    @pl.when(e == N_EXPERTS - 1)
    def _():
        y = x_ref[...] + gf_ref[...] * acc_ref[...]
        if final_norm:
            y = y * lax.rsqrt(jnp.mean(y * y, axis=-1, keepdims=True) + RMS_EPS) * fg_ref[...]
        o_ref[...] = y


def _experts(h, wd, x, gate, w_gate_up, w_down, sh_gate_up, sh_down, final_g, final_norm):
    bsz, seq, d = x.shape
    tm = min(seq, 1024)
    tok = lambda w: pl.BlockSpec((None, tm, w), lambda b, i, e: (b, i, 0))
    return pl.pallas_call(
        functools.partial(_experts_kernel, final_norm=final_norm),
        grid=(bsz, seq // tm, N_EXPERTS),
        in_specs=[tok(d), tok(N_EXPERTS), tok(d),
                  pl.BlockSpec((None, 1, d), lambda b, i, e: (b, 0, 0)),
                  pl.BlockSpec((None, d, 2 * EXPERT_DIM), lambda b, i, e: (e, 0, 0)),
                  pl.BlockSpec((None, EXPERT_DIM, d), lambda b, i, e: (e, 0, 0)),
                  pl.BlockSpec((d, 2 * EXPERT_DIM), lambda b, i, e: (0, 0)),
                  pl.BlockSpec((EXPERT_DIM, d), lambda b, i, e: (0, 0)),
                  pl.BlockSpec((1, d), lambda b, i, e: (0, 0))],
        out_specs=tok(d),
        out_shape=jax.ShapeDtypeStruct(x.shape, F32),
        scratch_shapes=[pltpu.VMEM((tm, d), F32)],
        compiler_params=_params("parallel", "parallel", "arbitrary"),
        name="moe_experts",
    )(h, wd, x, gate, w_gate_up, w_down, sh_gate_up, sh_down, final_g.reshape(1, d))


def _pad_w_in(w):
    cuts = (0, 1056, 1984, 3016, 4056)
    widths = (WA, WB, WC, WD)
    parts = []
    for lo, hi, wd in zip(cuts[:-1], cuts[1:], widths):
        parts.append(jnp.pad(w[:, lo:hi], ((0, 0), (0, wd - (hi - lo)))))
    return jnp.concatenate(parts, axis=1).astype(BF16)


def kernel(x, c, ctx, c_ctx, norm1_g, norm2_g, w_mod, b_mod, w_in, w_out, gla_gate_w2, gla_gate_b, gla_norm_g, rwkv_w_w2, rwkv_w0, rwkv_a_w2, rwkv_a0, rwkv_g_w2, rwkv_k_k, rwkv_k_a, rwkv_r_k, rwkv_ln_g, rwkv_ln_b, ssm_conv_w, ssm_conv_b, ssm_A_log, ssm_dt_bias, ssm_D, ssm_norm_g, dn_conv_w, dn_A_log, dn_dt_bias, dn_norm_g, router_w, router_bias, exp_w_gate, exp_w_up, exp_w_down, sh_w_gate, sh_w_up, sh_w_down, final_norm_g):
    bsz, n_lat, d = x.shape
    n_ctx = ctx.shape[1]
    depth = w_in.shape[0]
    rows = n_lat // GRID_W
    x = x.astype(F32)
    ctx = ctx.astype(F32)
    c_rows = jnp.concatenate([c, c_ctx[None, :], jnp.zeros((8 - bsz - 1, d), c.dtype)], axis=0).astype(F32)
    zero_init = (jnp.zeros((bsz, 2, N_HEADS, HEAD_DIM, HEAD_DIM), F32), jnp.zeros((bsz, 2, HW, HW), F32),
                 jnp.zeros((bsz, 2, N_HEADS, HEAD_DIM, SSM_STATE), F32), jnp.zeros((bsz, 2, HW, HW), F32))
    for i in range(depth):
        last = i == depth - 1
        lp = dict(gla_gate_w2=gla_gate_w2[i], gla_gate_b=gla_gate_b[i], rwkv_w_w2=rwkv_w_w2[i],
                  rwkv_w0=rwkv_w0[i], rwkv_a_w2=rwkv_a_w2[i], rwkv_a0=rwkv_a0[i],
                  rwkv_g_w2=rwkv_g_w2[i], rwkv_k_k=rwkv_k_k[i], rwkv_k_a=rwkv_k_a[i],
                  ssm_A_log=ssm_A_log[i], ssm_dt_bias=ssm_dt_bias[i],
                  dn_A_log=dn_A_log[i], dn_dt_bias=dn_dt_bias[i])
        mp = _mixer_params(lp)
        vecs = jnp.stack([gla_norm_g[i], rwkv_a0[i], rwkv_k_a[i], rwkv_r_k[i].reshape(GROUP),
                          rwkv_ln_g[i], rwkv_ln_b[i], jnp.repeat(ssm_D[i], HEAD_DIM), ssm_norm_g[i],
                          dn_norm_g[i]] + [jnp.zeros((GROUP,), F32)] * 7).astype(F32)
        w_in_pad = _pad_w_in(w_in[i])
        w_out_b = w_out[i].astype(BF16)
        w_gate_up = jnp.concatenate([exp_w_gate[i], exp_w_up[i]], axis=-1).astype(BF16)
        w_down = exp_w_down[i].astype(BF16)
        sh_gate_up = jnp.concatenate([sh_w_gate[i], sh_w_up[i]], axis=-1).astype(BF16)
        sh_down = sh_w_down[i].astype(BF16)
        dn_conv_b = jnp.zeros((3 * GROUP,), F32)

        mod = _modulation(c_rows, w_mod[i].astype(F32), b_mod[i].astype(F32))
        mod_x = [m[:, None, :] for m in jnp.split(mod[:bsz], 6, axis=-1)]
        mod_c = [jnp.broadcast_to(m[None], (bsz, 1, d)) for m in jnp.split(mod[bsz:bsz + 1], 6, axis=-1)]

        def mix(tokens, mods, init, g_rows, g_cols):
            proj = _in_projection(tokens, norm1_g[i], mods[0], mods[1], w_in_pad)
            convs = (_grid_conv(proj[2], GROUP, ssm_conv_w[i], ssm_conv_b[i], g_rows, g_cols),
                     _grid_conv(proj[3], 0, dn_conv_w[i], dn_conv_b, g_rows, g_cols))
            outs, states = _token_mixers(proj, convs, mp, init)
            return proj, convs, outs, states

        proj_c, convs_c, outs_c, ctx_states = mix(ctx, mod_c, zero_init, 1, n_ctx)
        proj_x, convs_x, outs_x, _ = mix(x, mod_x, ctx_states, rows, GRID_W)
        x = _out_projection(x, mod_x[2], outs_x, proj_x, convs_x[0][0], vecs,
                            mp['rwkv_wa'], mp['rwkv_wg'], w_out_b)

        def ffn(tokens, mods, final):
            h, wd = _route(tokens, norm2_g[i], mods[3], mods[4], router_w[i].astype(F32),
                           router_bias[i].astype(F32))
            return _experts(h, wd, tokens, mods[5], w_gate_up, w_down, sh_gate_up, sh_down,
                            final_norm_g.astype(F32), final)

        x = ffn(x, mod_x, last)
        if not last:
            ctx = _out_projection(ctx, mod_c[2], outs_c, proj_c, convs_c[0][0], vecs,
                                  mp['rwkv_wa'], mp['rwkv_wg'], w_out_b)
            ctx = ffn(ctx, mod_c, False)
    return x
```

```python
import functools

import jax
import jax.numpy as jnp
from jax import lax
from jax.experimental import pallas as pl
from jax.experimental.pallas import tpu as pltpu

F32 = jnp.float32
BF16 = jnp.bfloat16

D_MODEL = 1024
GROUP = 256
N_HEADS = 4
HEAD_DIM = 64
CHUNK = 64
GRID_W = 64
SSM_STATE = 128
N_EXPERTS = 64
TOP_K = 8
EXPERT_DIM = 256
ROUTED_SCALE = 2.5
RMS_EPS = 1e-6
RWKV_GN_EPS = 64e-5
GLA_GATE_NORM = 16.0
LANE = 128
CONV_HALO = 128
VMEM_LIMIT = 56 * 1024 * 1024

WA = 4 * GROUP + LANE
WB = 4 * GROUP
WC = 4 * GROUP + LANE
WD = 4 * GROUP + LANE
W_IN_PAD = WA + WB + WC + WD

NT = (((1,), (1,)), ((), ()))
TN = (((0,), (0,)), ((), ()))


def _dot(a, b):
    return jnp.dot(a, b, preferred_element_type=F32)


def _dot_nt(a, b):
    return lax.dot_general(a, b, NT, preferred_element_type=F32)


def _dot_tn(a, b):
    return lax.dot_general(a, b, TN, preferred_element_type=F32)


def _split3(x):
    hi = x.astype(BF16)
    r1 = x - hi.astype(F32)
    mid = r1.astype(BF16)
    lo = (r1 - mid.astype(F32)).astype(BF16)
    return hi, mid, lo


def _dot_sel_l(sel, x):
    hi, mid, lo = _split3(x)
    return _dot(sel, hi) + _dot(sel, mid) + _dot(sel, lo)


def _dot_sel_r(x, sel):
    hi, mid, lo = _split3(x)
    return _dot(hi, sel) + _dot(mid, sel) + _dot(lo, sel)


def _dot3(a, b):
    ah, am, _ = _split3(a)
    bh, bm, _ = _split3(b)
    return _dot(ah, bh) + (_dot(ah, bm) + _dot(am, bh))


def _softplus(x):
    return jnp.maximum(x, 0.0) + jnp.log1p(jnp.exp(-jnp.abs(x)))


def _sigmoid(x):
    return 1.0 / (1.0 + jnp.exp(-x))


def _silu(x):
    return x * _sigmoid(x)


def _head_ones():
    r = lax.broadcasted_iota(jnp.int32, (GROUP, GROUP), 0) // HEAD_DIM
    c = lax.broadcasted_iota(jnp.int32, (GROUP, GROUP), 1) // HEAD_DIM
    return jnp.where(r == c, 1.0, 0.0).astype(BF16)


def _head_sum(x):
    return _dot_sel_r(x, _head_ones())


def _params(*sem):
    return pltpu.CompilerParams(dimension_semantics=sem, vmem_limit_bytes=VMEM_LIMIT)


def _mod_kernel(c_ref, w_ref, b_ref, o_ref):
    o_ref[...] = _dot3(_silu(c_ref[...]), w_ref[...]) + b_ref[...]


def _modulation(c_rows, w, b):
    m, d = c_rows.shape
    n = w.shape[1]
    tn = 1536
    return pl.pallas_call(
        _mod_kernel,
        grid=(n // tn,),
        in_specs=[pl.BlockSpec((m, d), lambda j: (0, 0)),
                  pl.BlockSpec((d, tn), lambda j: (0, j)),
                  pl.BlockSpec((1, tn), lambda j: (0, j))],
        out_specs=pl.BlockSpec((m, tn), lambda j: (0, j)),
        out_shape=jax.ShapeDtypeStruct((m, n), F32),
        compiler_params=_params("parallel"),
        name="modulation",
    )(c_rows, w, b.reshape(1, n))


def _modulated_norm(x, g, shift, scale):
    y = x * lax.rsqrt(jnp.mean(x * x, axis=-1, keepdims=True) + RMS_EPS)
    return y * g * (1.0 + scale) + shift


def _inproj_kernel(x_ref, g_ref, sh_ref, sc_ref, w_ref, oa_ref, ob_ref, oc_ref, od_ref):
    h = _modulated_norm(x_ref[...], g_ref[...], sh_ref[...], sc_ref[...]).astype(BF16)
    y = _dot(h, w_ref[...])
    oa_ref[...] = y[:, :WA]
    ob_ref[...] = y[:, WA:WA + WB]
    oc_ref[...] = y[:, WA + WB:WA + WB + WC]
    od_ref[...] = y[:, WA + WB + WC:]


def _in_projection(x, g, shift, scale, w_pad):
    bsz, seq, d = x.shape
    tm = 256
    vec = pl.BlockSpec((None, 1, d), lambda b, i: (b, 0, 0))
    out = lambda w: pl.BlockSpec((None, tm, w), lambda b, i: (b, i, 0))
    shp = lambda w: jax.ShapeDtypeStruct((bsz, seq, w), F32)
    return pl.pallas_call(
        _inproj_kernel,
        grid=(bsz, seq // tm),
        in_specs=[pl.BlockSpec((None, tm, d), lambda b, i: (b, i, 0)),
                  pl.BlockSpec((1, d), lambda b, i: (0, 0)),
                  vec, vec,
                  pl.BlockSpec((d, W_IN_PAD), lambda b, i: (0, 0))],
        out_specs=[out(WA), out(WB), out(WC), out(WD)],
        out_shape=[shp(WA), shp(WB), shp(WC), shp(WD)],
        compiler_params=_params("parallel", "parallel"),
        name="in_projection",
    )(x, g.reshape(1, d), shift, scale, w_pad)


def _conv_kernel(prev_ref, cur_ref, next_ref, w_ref, b_ref, o_ref, ext_ref, *, cols, taps_r, n_tiles):
    i = pl.program_id(1)
    tm, ch = cur_ref.shape
    ext_ref[CONV_HALO:CONV_HALO + tm, :] = cur_ref[...]
    ext_ref[:CONV_HALO, :] = jnp.where(i > 0, prev_ref[...], 0.0)
    ext_ref[CONV_HALO + tm:, :] = jnp.where(i < n_tiles - 1, next_ref[...], 0.0)
    rt = 128
    for r0 in range(0, tm, rt):
        col = (lax.broadcasted_iota(jnp.int32, (rt, LANE), 0) + r0) % cols
        for c0 in range(0, ch, LANE):
            acc = jnp.zeros((rt, LANE), F32) + b_ref[:, c0:c0 + LANE]
            for dr in taps_r:
                for dc in (-1, 0, 1):
                    off = CONV_HALO + r0 + dr * cols + dc
                    tap = ext_ref[off:off + rt, c0:c0 + LANE]
                    if dc != 0:
                        tap = jnp.where((col + dc >= 0) & (col + dc < cols), tap, 0.0)
                    k = (dr + 1) * 3 + (dc + 1)
                    acc = acc + tap * w_ref[k:k + 1, c0:c0 + LANE]
            o_ref[r0:r0 + rt, c0:c0 + LANE] = _silu(acc)


def _grid_conv(proj, lane0, w, b, rows, cols):
    bsz, seq, _ = proj.shape
    tm = min(seq, 512)
    n_tiles = seq // tm
    hpt = tm // CONV_HALO
    n_halo = seq // CONV_HALO
    reach = cols + 1 if rows > 1 else 1
    assert seq == rows * cols and tm % cols == 0 and reach <= CONV_HALO and lane0 % GROUP == 0
    kern = functools.partial(_conv_kernel, cols=cols, taps_r=(-1, 0, 1) if rows > 1 else (0,),
                             n_tiles=n_tiles)
    w9 = w.reshape(9, 3 * GROUP)
    b1 = b.reshape(1, 3 * GROUP)
    outs = []
    for j in range(3):
        lb = lane0 // GROUP + j
        outs.append(pl.pallas_call(
            kern,
            grid=(bsz, n_tiles),
            in_specs=[
                pl.BlockSpec((None, CONV_HALO, GROUP),
                             lambda bi, i, lb=lb: (bi, jnp.maximum(i * hpt - 1, 0), lb)),
                pl.BlockSpec((None, tm, GROUP), lambda bi, i, lb=lb: (bi, i, lb)),
                pl.BlockSpec((None, CONV_HALO, GROUP),
                             lambda bi, i, lb=lb: (bi, jnp.minimum((i + 1) * hpt, n_halo - 1), lb)),
                pl.BlockSpec((9, GROUP), lambda bi, i, j=j: (0, j)),
                pl.BlockSpec((1, GROUP), lambda bi, i, j=j: (0, j))],
            out_specs=pl.BlockSpec((None, tm, GROUP), lambda bi, i: (bi, i, 0)),
            out_shape=jax.ShapeDtypeStruct((bsz, seq, GROUP), F32),
            scratch_shapes=[pltpu.VMEM((tm + 2 * CONV_HALO, GROUP), F32)],
            compiler_params=_params("parallel", "parallel"),
            name="grid_conv",
        )(proj, proj, proj, w9, b1))
    return outs


HW = N_HEADS * CHUNK


def _scan_prologue(s0_ref, s_scr):
    @pl.when(pl.program_id(1) == 0)
    def _():
        s_scr[...] = s0_ref[...]


def _scan_epilogue(sfin_ref, s_scr):
    @pl.when(pl.program_id(1) == pl.num_programs(1) - 1)
    def _():
        sfin_ref[...] = s_scr[...]


def _token_masks(direction, width):
    r = lax.broadcasted_iota(jnp.int32, (CHUNK, width), 0)
    c = lax.broadcasted_iota(jnp.int32, (CHUNK, width), 1) % CHUNK
    diff = (r - c) * (1 - 2 * direction)
    return diff >= 0, diff > 0


def _stacked_masks(direction):
    r = lax.broadcasted_iota(jnp.int32, (HW, HW), 0)
    c = lax.broadcasted_iota(jnp.int32, (HW, HW), 1)
    same = (r // CHUNK) == (c // CHUNK)
    diff = (r - c) * (1 - 2 * direction)
    return same, same & (diff >= 0), same & (diff > 0), r == c


def _tile4(x):
    return jnp.concatenate([x] * N_HEADS, axis=0)


def _stack4(x, same):
    return jnp.where(same, _tile4(x), 0.0)


def _unstack4(y):
    return (y[:CHUNK] + y[CHUNK:2 * CHUNK]) + (y[2 * CHUNK:3 * CHUNK] + y[3 * CHUNK:])


def _neumann_inverse_many(ns, eye):
    ts = [eye + n for n in ns]
    ps = [n.astype(BF16) for n in ns]
    for _ in range(5):
        ps = [_dot(p, p).astype(BF16) for p in ps]
        ts = [t + _dot(t.astype(BF16), p) for t, p in zip(ts, ps)]
    return ts


def _cum_rows_cols(g, incl_f, incl_t_f):
    cc = _dot_sel_l(incl_f, g)
    ones = jnp.ones((CHUNK, CHUNK), BF16)
    rr = _dot_sel_l(ones, g * jnp.concatenate([incl_t_f] * N_HEADS, axis=1))
    return cc, rr


def _gla_kernel(qkv_f, tail_f, qkv_b, tail_b, w2_ref, b_ref, s0_ref, of_ref, ob_ref, sfin_ref, s_scr):
    _scan_prologue(s0_ref, s_scr)
    dirs = (0, 1)
    prep = []
    for d, (qkv_ref, tail_ref) in zip(dirs, ((qkv_f, tail_f), (qkv_b, tail_b))):
        same, incl4, _, _ = _stacked_masks(d)
        incl_t, _ = _token_masks(d, CHUNK)
        p = qkv_ref[...]
        q = p[:, :GROUP] * HEAD_DIM ** -0.5
        k = p[:, GROUP:2 * GROUP]
        v = p[:, 2 * GROUP:]
        zg = _dot(tail_ref[...], w2_ref[d]) + b_ref[d]
        gk = (jnp.minimum(zg, 0.0) - jnp.log1p(jnp.exp(-jnp.abs(zg)))) / GLA_GATE_NORM
        bc = _dot_sel_l(jnp.where(incl_t, 1.0, 0.0).astype(BF16), gk)
        bt = jnp.sum(gk, axis=0, keepdims=True)
        prep.append(dict(
            same=same, incl4=incl4,
            qt4=_stack4(q * jnp.exp(bc), same).astype(BF16),
            kt=_tile4(k * jnp.exp(-bc)).astype(BF16),
            ks=_tile4(k * jnp.exp(bt - bc)).astype(BF16),
            v4=_stack4(v, same).astype(BF16), dec=jnp.exp(bt)))
    att = [jnp.where(c['incl4'], _dot_nt(c['qt4'], c['kt']), 0.0).astype(BF16) for c in prep]
    sts = [s_scr[d] for d in dirs]
    o = [_dot(m, c['v4']) + _dot_nt(c['qt4'], s.astype(BF16)) for m, c, s in zip(att, prep, sts)]
    for d, o_ref in zip(dirs, (of_ref, ob_ref)):
        c = prep[d]
        o_ref[...] = _unstack4(o[d])
        s_scr[d] = sts[d] * c['dec'] + jnp.where(c['same'], _dot_tn(c['v4'], c['ks']), 0.0)
    _scan_epilogue(sfin_ref, s_scr)


def _ssd_kernel(tail_f, xs_f, bm_f, cm_f, tail_b, xs_b, bm_b, cm_b, e_ref, dtb_ref, alog_ref, s0_ref,
                of_ref, ob_ref, sfin_ref, s_scr):
    _scan_prologue(s0_ref, s_scr)
    dirs = (0, 1)
    prep = []
    for d, (tail_ref, xs_ref, bm_ref, cm_ref) in zip(
            dirs, ((tail_f, xs_f, bm_f, cm_f), (tail_b, xs_b, bm_b, cm_b))):
        same, _, _, _ = _stacked_masks(d)
        incl_t, _ = _token_masks(d, GROUP)
        incl_c, _ = _token_masks(d, CHUNK)
        incl_f = jnp.where(incl_c, 1.0, 0.0)
        dt = _softplus(_dot_sel_r(tail_ref[...], e_ref[d]) + dtb_ref[d])
        a = -jnp.exp(alog_ref[d]) * dt
        x = xs_ref[...] * dt
        cc, rr = _cum_rows_cols(a, incl_f.astype(BF16), incl_f.T)
        at = jnp.sum(a, axis=0, keepdims=True)
        bm = bm_ref[...]
        cm = cm_ref[...]
        by_head = lambda t: jnp.concatenate(
            [t[:, (h // 2) * SSM_STATE:(h // 2 + 1) * SSM_STATE] for h in range(N_HEADS)], axis=0)
        prep.append(dict(
            lm4=_stack4(jnp.exp(jnp.where(incl_t, cc - rr, -jnp.inf)), same),
            bm4=by_head(bm).astype(BF16), cm4=by_head(cm).astype(BF16),
            x4=_stack4(x, same).astype(BF16),
            xd4=_stack4(x * jnp.exp(at - cc), same).astype(BF16),
            e4=_stack4(jnp.exp(cc), same),
            dec4=jnp.concatenate(
                [jnp.broadcast_to(jnp.exp(at[:, h * HEAD_DIM:h * HEAD_DIM + 1]), (CHUNK, SSM_STATE))
                 for h in range(N_HEADS)], axis=0)))
    cb = [(_dot_nt(c['cm4'], c['bm4']) * c['lm4']).astype(BF16) for c in prep]
    sts = [s_scr[d] for d in dirs]
    y = [_dot(m, c['x4']) + _dot_nt(c['cm4'], s.astype(BF16)) * c['e4'] for m, c, s in zip(cb, prep, sts)]
    for d, o_ref in zip(dirs, (of_ref, ob_ref)):
        c = prep[d]
        o_ref[...] = _unstack4(y[d])
        s_scr[d] = sts[d] * c['dec4'] + _dot_tn(c['xd4'], c['bm4'])
    _scan_epilogue(sfin_ref, s_scr)


def _rwkv_kernel(p_f, p_b, w0_ref, ww_ref, a0_ref, wa_ref, kk_ref, ka_ref, s0_ref,
                 of_ref, ob_ref, sfin_ref, s_scr):
    _scan_prologue(s0_ref, s_scr)
    dirs = (0, 1)
    prep = []
    for d, p_ref in zip(dirs, (p_f, p_b)):
        same, incl4, strict4, diag4 = _stacked_masks(d)
        incl_t, _ = _token_masks(d, CHUNK)
        p = p_ref[...]
        r = p[:, :GROUP]
        k = p[:, GROUP:2 * GROUP]
        v = p[:, 2 * GROUP:3 * GROUP]
        tail = p[:, 3 * GROUP:]
        wr = w0_ref[d] + _dot(jnp.tanh(tail), ww_ref[d])
        lw = -jnp.exp(-_softplus(-wr) - 0.5)
        a = _sigmoid(a0_ref[...] + _dot(tail, wa_ref[...]))
        kk = k * kk_ref[...]
        kk = kk * lax.rsqrt(_head_sum(kk * kk) + 1e-6)
        kmod = k * (1.0 + (a - 1.0) * ka_ref[...])
        bv = kk * a
        pc = _dot_sel_l(jnp.where(incl_t, 1.0, 0.0).astype(BF16), lw)
        pt = jnp.sum(lw, axis=0, keepdims=True)
        e_npc = jnp.exp(-pc)
        e_rest = jnp.exp(pt - pc)
        zr = jnp.concatenate([_stack4(-kk * jnp.exp(pc - lw), same), _stack4(r * jnp.exp(pc), same)],
                             axis=0).astype(BF16)
        prep.append(dict(
            same=same, incl4=incl4, strict4=strict4, zr=zr,
            bt=_tile4(bv * e_npc).astype(BF16), kt=_tile4(kmod * e_npc).astype(BF16),
            v4=_stack4(v, same).astype(BF16),
            uvb=jnp.concatenate([_tile4(bv * e_rest), _tile4(kmod * e_rest)], axis=0).astype(BF16),
            dec=jnp.exp(pt), eye=jnp.where(diag4, 1.0, 0.0)))
    ab = [_dot_nt(c['zr'], c['bt']) for c in prep]
    ak = [_dot_nt(c['zr'], c['kt']) for c in prep]
    a_zb = [jnp.where(c['strict4'], m[:HW], 0.0) for c, m in zip(prep, ab)]
    a_zk = [jnp.where(c['strict4'], m[:HW], 0.0).astype(BF16) for c, m in zip(prep, ak)]
    a_rb = [jnp.where(c['incl4'], m[HW:], 0.0).astype(BF16) for c, m in zip(prep, ab)]
    a_rk = [jnp.where(c['incl4'], m[HW:], 0.0).astype(BF16) for c, m in zip(prep, ak)]
    ts = [t.astype(BF16) for t in _neumann_inverse_many(a_zb, prep[0]['eye'])]
    sts = [s_scr[d] for d in dirs]
    stb = [s.astype(BF16) for s in sts]
    wu = [_dot(t, c['zr'][:HW]) for t, c in zip(ts, prep)]
    zkv = [_dot(m, c['v4']) for m, c in zip(a_zk, prep)]
    uv = [_dot(t, m.astype(BF16)) for t, m in zip(ts, zkv)]
    u = [_dot_nt(w.astype(BF16), s) + x for w, s, x in zip(wu, stb, uv)]
    ub = [x.astype(BF16) for x in u]
    y = [_dot_nt(c['zr'][HW:], s) + _dot(m1, x) + _dot(m2, c['v4'])
         for c, s, m1, x, m2 in zip(prep, stb, a_rb, ub, a_rk)]
    for d, o_ref in zip(dirs, (of_ref, ob_ref)):
        c = prep[d]
        o_ref[...] = _unstack4(y[d])
        upd = _dot_tn(jnp.concatenate([ub[d], c['v4']], axis=0), c['uvb'])
        s_scr[d] = sts[d] * c['dec'] + jnp.where(c['same'], upd, 0.0)
    _scan_epilogue(sfin_ref, s_scr)


def _gdn_kernel(tail_f, q_f, k_f, v_f, tail_b, q_b, k_b, v_b, eb_ref, ea_ref, dtb_ref, alog_ref, s0_ref,
                of_ref, ob_ref, sfin_ref, s_scr):
    _scan_prologue(s0_ref, s_scr)
    dirs = (0, 1)
    prep = []
    for d, (tail_ref, q_ref, k_ref, v_ref) in zip(dirs, ((tail_f, q_f, k_f, v_f), (tail_b, q_b, k_b, v_b))):
        same, _, _, diag4 = _stacked_masks(d)
        incl_t, strict_t = _token_masks(d, GROUP)
        incl_c, _ = _token_masks(d, CHUNK)
        incl_f = jnp.where(incl_c, 1.0, 0.0)
        q = q_ref[...]
        q = q * lax.rsqrt(_head_sum(q * q) + 1e-6) * HEAD_DIM ** -0.5
        k = k_ref[...]
        k = k * lax.rsqrt(_head_sum(k * k) + 1e-6)
        v = v_ref[...]
        tail = tail_ref[...]
        beta = _sigmoid(_dot_sel_r(tail, eb_ref[d]))
        g = -jnp.exp(alog_ref[d]) * _softplus(_dot_sel_r(tail, ea_ref[d]) + dtb_ref[d])
        cc, rr = _cum_rows_cols(g, incl_f.astype(BF16), incl_f.T)
        seg = cc - rr
        gt = jnp.sum(g, axis=0, keepdims=True)
        e_cc = jnp.exp(cc)
        kb = k * beta
        prep.append(dict(
            same=same,
            d_strict=_stack4(jnp.exp(jnp.where(strict_t, seg, -jnp.inf)), same),
            d_incl=_stack4(jnp.exp(jnp.where(incl_t, seg, -jnp.inf)), same),
            kb4=_stack4(kb, same).astype(BF16), kt=_tile4(k).astype(BF16),
            q4=_stack4(q, same).astype(BF16),
            rhs=jnp.concatenate([_stack4(kb * e_cc, same), _stack4(v * beta, same)], axis=1).astype(BF16),
            qd4=_stack4(q * e_cc, same).astype(BF16),
            kd4=_stack4(k * jnp.exp(gt - cc), same).astype(BF16),
            dec=jnp.exp(gt), eye=jnp.where(diag4, 1.0, 0.0)))
    a_kk = [_dot_nt(c['kb4'], c['kt']) * c['d_strict'] for c in prep]
    a_qk = [(_dot_nt(c['q4'], c['kt']) * c['d_incl']).astype(BF16) for c in prep]
    ts = [t.astype(BF16) for t in _neumann_inverse_many([-a for a in a_kk], prep[0]['eye'])]
    sol = [_dot(t, c['rhs']) for t, c in zip(ts, prep)]
    sts = [s_scr[d] for d in dirs]
    stb = [s.astype(BF16) for s in sts]
    v_new = [(x[:, HW:] - _dot(x[:, :HW].astype(BF16), s)).astype(BF16) for x, s in zip(sol, stb)]
    o = [_dot(c['qd4'], s) + _dot(m, vn) for c, s, m, vn in zip(prep, stb, a_qk, v_new)]
    for d, o_ref in zip(dirs, (of_ref, ob_ref)):
        c = prep[d]
        o_ref[...] = _unstack4(o[d])
        s_scr[d] = sts[d] * c['dec'] + _dot_tn(c['kd4'], v_new[d])
    _scan_epilogue(sfin_ref, s_scr)


def _scan_call(kern, name, data, params, init):
    bsz, seq = data[0][0].shape[:2]
    n_chunks = seq // CHUNK
    in_specs, args = [], []
    for mirror in (False, True):
        for arr, width, blk in data:
            idx = (lambda b, n, blk=blk: (b, n_chunks - 1 - n, blk)) if mirror else \
                  (lambda b, n, blk=blk: (b, n, blk))
            in_specs.append(pl.BlockSpec((None, CHUNK, width), idx))
            args.append(arr)
    for arr in params:
        in_specs.append(pl.BlockSpec(arr.shape, lambda b, n, nd=arr.ndim: (0,) * nd))
        args.append(arr)
    st_shape = init.shape[1:]
    st_spec = pl.BlockSpec((None,) + st_shape, lambda b, n, nd=len(st_shape): (b,) + (0,) * nd)
    in_specs.append(st_spec)
    args.append(init)
    o_shape = jax.ShapeDtypeStruct((bsz, seq, GROUP), F32)
    return pl.pallas_call(
        kern,
        grid=(bsz, n_chunks),
        in_specs=in_specs,
        out_specs=[pl.BlockSpec((None, CHUNK, GROUP), lambda b, n: (b, n, 0)),
                   pl.BlockSpec((None, CHUNK, GROUP), lambda b, n: (b, n_chunks - 1 - n, 0)),
                   st_spec],
        out_shape=[o_shape, o_shape, jax.ShapeDtypeStruct(init.shape, F32)],
        scratch_shapes=[pltpu.VMEM(st_shape, F32)],
        compiler_params=_params("parallel", "arbitrary"),
        name=name,
    )(*args)


def _expand_matrix(row0):
    d = jnp.arange(2)[:, None, None]
    r = jnp.arange(LANE)[None, :, None]
    c = jnp.arange(GROUP)[None, None, :]
    return (r == row0 + N_HEADS * d + c // HEAD_DIM).astype(BF16)


def _per_head_lanes(t):
    return jnp.repeat(t.astype(F32), HEAD_DIM, axis=-1)[:, None, :]


def _rows_at(w, row0, total):
    pad = [(0, 0)] * (w.ndim - 2) + [(row0, total - row0 - w.shape[-2]), (0, 0)]
    return jnp.pad(w.astype(F32), pad)


def _mixer_params(lp):
    f = lambda t: t.astype(F32).reshape(1, GROUP)
    return dict(
        gla_w2=jnp.stack([_rows_at(lp['gla_gate_w2'][0], 0, LANE), _rows_at(lp['gla_gate_w2'][1], 16, LANE)]),
        gla_b=lp['gla_gate_b'].astype(F32)[:, None, :],
        rwkv_w0=lp['rwkv_w0'].astype(F32)[:, None, :],
        rwkv_ww=jnp.stack([_rows_at(lp['rwkv_w_w2'][0], 0, GROUP), _rows_at(lp['rwkv_w_w2'][1], 32, GROUP)]),
        rwkv_a0=f(lp['rwkv_a0']),
        rwkv_wa=_rows_at(lp['rwkv_a_w2'], 64, GROUP),
        rwkv_wg=_rows_at(lp['rwkv_g_w2'], 96, GROUP),
        rwkv_kk=f(lp['rwkv_k_k']),
        rwkv_ka=f(lp['rwkv_k_a']),
        ssd_e=_expand_matrix(0),
        ssd_dtb=_per_head_lanes(lp['ssm_dt_bias']),
        ssd_alog=_per_head_lanes(lp['ssm_A_log']),
        gdn_eb=_expand_matrix(0),
        gdn_ea=_expand_matrix(8),
        gdn_dtb=_per_head_lanes(lp['dn_dt_bias']),
        gdn_alog=_per_head_lanes(lp['dn_A_log']),
    )


def _token_mixers(proj, convs, mp, init):
    pa, pb, pc, pd = proj
    (xs, bm, cm), (dq, dk, dv) = convs
    tail_blk = 4 * GROUP // LANE
    ga = _scan_call(_gla_kernel, "gla_scan", [(pa, 3 * GROUP, 0), (pa, LANE, tail_blk)],
                    [mp['gla_w2'], mp['gla_b']], init[0])
    rb = _scan_call(_rwkv_kernel, "rwkv_scan", [(pb, WB, 0)],
                    [mp['rwkv_w0'], mp['rwkv_ww'], mp['rwkv_a0'], mp['rwkv_wa'], mp['rwkv_kk'],
                     mp['rwkv_ka']], init[1])
    sc = _scan_call(_ssd_kernel, "ssd_scan",
                    [(pc, LANE, tail_blk), (xs, GROUP, 0), (bm, GROUP, 0), (cm, GROUP, 0)],
                    [mp['ssd_e'], mp['ssd_dtb'], mp['ssd_alog']], init[2])
    gd = _scan_call(_gdn_kernel, "gdn_scan",
                    [(pd, LANE, tail_blk), (dq, GROUP, 0), (dk, GROUP, 0), (dv, GROUP, 0)],
                    [mp['gdn_eb'], mp['gdn_ea'], mp['gdn_dtb'], mp['gdn_alog']], init[3])
    outs = tuple(o for call in (ga, rb, sc, gd) for o in call[:2])
    return outs, (ga[2], rb[2], sc[2], gd[2])


V_GLA_G, V_A0, V_KA, V_RK, V_LN_G, V_LN_B, V_SSM_D, V_SSM_G, V_DN_G = range(9)


def _group_ones():
    r = lax.broadcasted_iota(jnp.int32, (GROUP, GROUP), 0) // SSM_STATE
    c = lax.broadcasted_iota(jnp.int32, (GROUP, GROUP), 1) // SSM_STATE
    return jnp.where(r == c, 1.0, 0.0).astype(BF16)


def _outproj_kernel(x_ref, ga_ref, oaf_ref, oab_ref, obf_ref, obb_ref, ocf_ref, ocb_ref, odf_ref, odb_ref,
                    ag_ref, pb_ref, cz_ref, xs_ref, dz_ref, vec_ref, wa_ref, wg_ref, wout_ref, o_ref):
    vec = lambda i: vec_ref[i:i + 1, :]
    inv_hd = 1.0 / HEAD_DIM

    o = oaf_ref[...] + oab_ref[...]
    ya = o * lax.rsqrt(_head_sum(o * o) * inv_hd + RMS_EPS) * vec(V_GLA_G) * _silu(ag_ref[...])

    pb = pb_ref[...]
    r = pb[:, :GROUP]
    k = pb[:, GROUP:2 * GROUP]
    v = pb[:, 2 * GROUP:3 * GROUP]
    tail = pb[:, 3 * GROUP:]
    a = _sigmoid(vec(V_A0) + _dot(tail, wa_ref[...]))
    kmod = k * (1.0 + (a - 1.0) * vec(V_KA))
    gate = _dot(_sigmoid(tail), wg_ref[...])
    o = obf_ref[...] + obb_ref[...]
    dev = o - _head_sum(o) * inv_hd
    on = dev * lax.rsqrt(_head_sum(dev * dev) * inv_hd + RWKV_GN_EPS) * vec(V_LN_G) + vec(V_LN_B)
    yb = (on + _head_sum(r * kmod * vec(V_RK)) * v) * gate

    y = (ocf_ref[...] + ocb_ref[...] + vec(V_SSM_D) * xs_ref[...]) * _silu(cz_ref[...])
    yc = y * lax.rsqrt(_dot_sel_r(y * y, _group_ones()) * (1.0 / SSM_STATE) + RMS_EPS) * vec(V_SSM_G)

    o = odf_ref[...] + odb_ref[...]
    yd = o * lax.rsqrt(_head_sum(o * o) * inv_hd + RMS_EPS) * vec(V_DN_G) * _silu(dz_ref[...])

    cat = jnp.concatenate([ya, yb, yc, yd], axis=1).astype(BF16)
    o_ref[...] = x_ref[...] + ga_ref[...] * _dot(cat, wout_ref[...])


def _out_projection(x, gate, outs, proj, xs, vecs, wa, wg, w_out):
    bsz, seq, d = x.shape
    tm = 256
    pa, pb, pc, pd = proj
    col = lambda blk: pl.BlockSpec((None, tm, GROUP), lambda b, i: (b, i, blk))
    full = lambda arr: pl.BlockSpec(arr.shape, lambda b, i: (0, 0))
    return pl.pallas_call(
        _outproj_kernel,
        grid=(bsz, seq // tm),
        in_specs=[pl.BlockSpec((None, tm, d), lambda b, i: (b, i, 0)),
                  pl.BlockSpec((None, 1, d), lambda b, i: (b, 0, 0)),
                  col(0), col(0), col(0), col(0), col(0), col(0), col(0), col(0),
                  col(3),
                  pl.BlockSpec((None, tm, WB), lambda b, i: (b, i, 0)),
                  col(0), col(0), col(3),
                  full(vecs), full(wa), full(wg), full(w_out)],
        out_specs=pl.BlockSpec((None, tm, d), lambda b, i: (b, i, 0)),
        out_shape=jax.ShapeDtypeStruct(x.shape, F32),
        compiler_params=_params("parallel", "parallel"),
        name="out_projection",
    )(x, gate, *outs, pa, pb, pc, xs, pd, vecs, wa, wg, w_out)


def _route_kernel(x_ref, g_ref, sh_ref, sc_ref, rw_ref, rb_ref, h_ref, wd_ref):
    h = _modulated_norm(x_ref[...], g_ref[...], sh_ref[...], sc_ref[...])
    h_ref[...] = h.astype(BF16)
    scores = _sigmoid(_dot3(h, rw_ref[...]))
    sel = scores + rb_ref[...]
    lane = lax.broadcasted_iota(jnp.int32, sel.shape, 1)
    chosen = jnp.zeros(sel.shape, F32)
    for _ in range(TOP_K):
        m = jnp.max(sel, axis=-1, keepdims=True)
        first = jnp.min(jnp.where(sel == m, lane, N_EXPERTS), axis=-1, keepdims=True)
        pick = lane == first
        chosen = jnp.where(pick, 1.0, chosen)
        sel = jnp.where(pick, -jnp.inf, sel)
    w = scores * chosen
    wd_ref[...] = w / jnp.sum(w, axis=-1, keepdims=True) * ROUTED_SCALE


def _route(x, g, shift, scale, router_w, router_bias):
    bsz, seq, d = x.shape
    tm = 256
    vec = pl.BlockSpec((None, 1, d), lambda b, i: (b, 0, 0))
    return pl.pallas_call(
        _route_kernel,
        grid=(bsz, seq // tm),
        in_specs=[pl.BlockSpec((None, tm, d), lambda b, i: (b, i, 0)),
                  pl.BlockSpec((1, d), lambda b, i: (0, 0)),
                  vec, vec,
                  pl.BlockSpec((d, N_EXPERTS), lambda b, i: (0, 0)),
                  pl.BlockSpec((1, N_EXPERTS), lambda b, i: (0, 0))],
        out_specs=[pl.BlockSpec((None, tm, d), lambda b, i: (b, i, 0)),
                   pl.BlockSpec((None, tm, N_EXPERTS), lambda b, i: (b, i, 0))],
        out_shape=[jax.ShapeDtypeStruct((bsz, seq, d), BF16),
                   jax.ShapeDtypeStruct((bsz, seq, N_EXPERTS), F32)],
        compiler_params=_params("parallel", "parallel"),
        name="moe_route",
    )(x, g.reshape(1, d), shift, scale, router_w, router_bias.reshape(1, N_EXPERTS))


def _swiglu_hidden(h, w_gate_up):
    gu = _dot(h, w_gate_up)
    return _silu(gu[:, :EXPERT_DIM]) * gu[:, EXPERT_DIM:]


def _experts_kernel(h_ref, wd_ref, x_ref, gf_ref, wgu_ref, wdn_ref, sgu_ref, sdn_ref, fg_ref,
                    o_ref, acc_ref, *, final_norm):
    grp = pl.program_id(2)
    per_step = wgu_ref.shape[0]
    h = h_ref[...]

    @pl.when(grp == 0)
    def _():
        acc_ref[...] = _dot(_swiglu_hidden(h, sgu_ref[...]).astype(BF16), sdn_ref[...])

    wd = wd_ref[...]
    lane = lax.broadcasted_iota(jnp.int32, wd.shape, 1)
    acts = []
    for j in range(per_step):
        w_e = jnp.sum(jnp.where(lane == grp * per_step + j, wd, 0.0), axis=-1, keepdims=True)
        acts.append((_swiglu_hidden(h, wgu_ref[j]) * w_e).astype(BF16))
    w_down = wdn_ref[...].reshape(per_step * EXPERT_DIM, wdn_ref.shape[-1])
    acc_ref[...] += _dot(jnp.concatenate(acts, axis=1), w_down)

    @pl.when(grp == pl.num_programs(2) - 1)
    def _():
        y = x_ref[...] + gf_ref[...] * acc_ref[...]
        if final_norm:
            y = y * lax.rsqrt(jnp.mean(y * y, axis=-1, keepdims=True) + RMS_EPS) * fg_ref[...]
        o_ref[...] = y


EXPERTS_PER_STEP = 4


def _experts(h, wd, x, gate, w_gate_up, w_down, sh_gate_up, sh_down, final_g, final_norm):
    bsz, seq, d = x.shape
    tm = min(seq, 1024)
    eps = EXPERTS_PER_STEP
    tok = lambda w: pl.BlockSpec((None, tm, w), lambda b, i, e: (b, i, 0))
    return pl.pallas_call(
        functools.partial(_experts_kernel, final_norm=final_norm),
        grid=(bsz, seq // tm, N_EXPERTS // eps),
        in_specs=[tok(d), tok(N_EXPERTS), tok(d),
                  pl.BlockSpec((None, 1, d), lambda b, i, e: (b, 0, 0)),
                  pl.BlockSpec((eps, d, 2 * EXPERT_DIM), lambda b, i, e: (e, 0, 0)),
                  pl.BlockSpec((eps, EXPERT_DIM, d), lambda b, i, e: (e, 0, 0)),
                  pl.BlockSpec((d, 2 * EXPERT_DIM), lambda b, i, e: (0, 0)),
                  pl.BlockSpec((EXPERT_DIM, d), lambda b, i, e: (0, 0)),
                  pl.BlockSpec((1, d), lambda b, i, e: (0, 0))],
        out_specs=tok(d),
        out_shape=jax.ShapeDtypeStruct(x.shape, F32),
        scratch_shapes=[pltpu.VMEM((tm, d), F32)],
        compiler_params=_params("parallel", "parallel", "arbitrary"),
        name="moe_experts",
    )(h, wd, x, gate, w_gate_up, w_down, sh_gate_up, sh_down, final_g.reshape(1, d))


def _pad_w_in(w):
    cuts = (0, 1056, 1984, 3016, 4056)
    widths = (WA, WB, WC, WD)
    parts = []
    for lo, hi, wd in zip(cuts[:-1], cuts[1:], widths):
        parts.append(jnp.pad(w[:, lo:hi], ((0, 0), (0, wd - (hi - lo)))))
    return jnp.concatenate(parts, axis=1).astype(BF16)


def kernel(x, c, ctx, c_ctx, norm1_g, norm2_g, w_mod, b_mod, w_in, w_out, gla_gate_w2, gla_gate_b, gla_norm_g, rwkv_w_w2, rwkv_w0, rwkv_a_w2, rwkv_a0, rwkv_g_w2, rwkv_k_k, rwkv_k_a, rwkv_r_k, rwkv_ln_g, rwkv_ln_b, ssm_conv_w, ssm_conv_b, ssm_A_log, ssm_dt_bias, ssm_D, ssm_norm_g, dn_conv_w, dn_A_log, dn_dt_bias, dn_norm_g, router_w, router_bias, exp_w_gate, exp_w_up, exp_w_down, sh_w_gate, sh_w_up, sh_w_down, final_norm_g):
    bsz, n_lat, d = x.shape
    n_ctx = ctx.shape[1]
    depth = w_in.shape[0]
    rows = n_lat // GRID_W
    x = x.astype(F32)
    ctx = ctx.astype(F32)
    c_rows = jnp.concatenate([c, c_ctx[None, :], jnp.zeros((8 - bsz - 1, d), c.dtype)], axis=0).astype(F32)
    zero_init = (jnp.zeros((bsz, 2, HW, HW), F32), jnp.zeros((bsz, 2, HW, HW), F32),
                 jnp.zeros((bsz, 2, HW, SSM_STATE), F32), jnp.zeros((bsz, 2, HW, HW), F32))
    for i in range(depth):
        last = i == depth - 1
        lp = dict(gla_gate_w2=gla_gate_w2[i], gla_gate_b=gla_gate_b[i], rwkv_w_w2=rwkv_w_w2[i],
                  rwkv_w0=rwkv_w0[i], rwkv_a_w2=rwkv_a_w2[i], rwkv_a0=rwkv_a0[i],
                  rwkv_g_w2=rwkv_g_w2[i], rwkv_k_k=rwkv_k_k[i], rwkv_k_a=rwkv_k_a[i],
                  ssm_A_log=ssm_A_log[i], ssm_dt_bias=ssm_dt_bias[i],
                  dn_A_log=dn_A_log[i], dn_dt_bias=dn_dt_bias[i])
        mp = _mixer_params(lp)
        vecs = jnp.stack([gla_norm_g[i], rwkv_a0[i], rwkv_k_a[i], rwkv_r_k[i].reshape(GROUP),
                          rwkv_ln_g[i], rwkv_ln_b[i], jnp.repeat(ssm_D[i], HEAD_DIM), ssm_norm_g[i],
                          dn_norm_g[i]] + [jnp.zeros((GROUP,), F32)] * 7).astype(F32)
        w_in_pad = _pad_w_in(w_in[i])
        w_out_b = w_out[i].astype(BF16)
        w_gate_up = jnp.concatenate([exp_w_gate[i], exp_w_up[i]], axis=-1).astype(BF16)
        w_down = exp_w_down[i].astype(BF16)
        sh_gate_up = jnp.concatenate([sh_w_gate[i], sh_w_up[i]], axis=-1).astype(BF16)
        sh_down = sh_w_down[i].astype(BF16)
        dn_conv_b = jnp.zeros((3 * GROUP,), F32)

        mod = _modulation(c_rows, w_mod[i].astype(F32), b_mod[i].astype(F32))
        mod_x = [m[:, None, :] for m in jnp.split(mod[:bsz], 6, axis=-1)]
        mod_c = [jnp.broadcast_to(m[None], (bsz, 1, d)) for m in jnp.split(mod[bsz:bsz + 1], 6, axis=-1)]

        def mix(tokens, mods, init, g_rows, g_cols):
            proj = _in_projection(tokens, norm1_g[i], mods[0], mods[1], w_in_pad)
            convs = (_grid_conv(proj[2], GROUP, ssm_conv_w[i], ssm_conv_b[i], g_rows, g_cols),
                     _grid_conv(proj[3], 0, dn_conv_w[i], dn_conv_b, g_rows, g_cols))
            outs, states = _token_mixers(proj, convs, mp, init)
            return proj, convs, outs, states

        proj_c, convs_c, outs_c, ctx_states = mix(ctx, mod_c, zero_init, 1, n_ctx)
        proj_x, convs_x, outs_x, _ = mix(x, mod_x, ctx_states, rows, GRID_W)
        x = _out_projection(x, mod_x[2], outs_x, proj_x, convs_x[0][0], vecs,
                            mp['rwkv_wa'], mp['rwkv_wg'], w_out_b)

        def ffn(tokens, mods, final, flatten=False):
            shape = tokens.shape
            if flatten:
                tokens = tokens.reshape(1, shape[0] * shape[1], d)
                mods = [m[:1] for m in mods]
            h, wd = _route(tokens, norm2_g[i], mods[3], mods[4], router_w[i].astype(F32),
                           router_bias[i].astype(F32))
            out = _experts(h, wd, tokens, mods[5], w_gate_up, w_down, sh_gate_up, sh_down,
                           final_norm_g.astype(F32), final)
            return out.reshape(shape)

        x = ffn(x, mod_x, last)
        if not last:
            ctx = _out_projection(ctx, mod_c[2], outs_c, proj_c, convs_c[0][0], vecs,
                                  mp['rwkv_wa'], mp['rwkv_wg'], w_out_b)
            ctx = ffn(ctx, mod_c, False, flatten=True)
    return x
```

```python
import functools

import jax
import jax.numpy as jnp
from jax import lax
from jax.experimental import pallas as pl
from jax.experimental.pallas import tpu as pltpu

F32 = jnp.float32
BF16 = jnp.bfloat16

D_MODEL = 1024
GROUP = 256
N_HEADS = 4
HEAD_DIM = 64
CHUNK = 64
GRID_W = 64
SSM_STATE = 128
N_EXPERTS = 64
TOP_K = 8
EXPERT_DIM = 256
ROUTED_SCALE = 2.5
RMS_EPS = 1e-6
RWKV_GN_EPS = 64e-5
GLA_GATE_NORM = 16.0
LANE = 128
CONV_HALO = 128
VMEM_LIMIT = 56 * 1024 * 1024

WA = 4 * GROUP + LANE
WB = 4 * GROUP
WC = 4 * GROUP + LANE
WD = 4 * GROUP + LANE
W_IN_PAD = WA + WB + WC + WD

NT = (((1,), (1,)), ((), ()))
TN = (((0,), (0,)), ((), ()))


def _dot(a, b):
    return jnp.dot(a, b, preferred_element_type=F32)


def _dot_nt(a, b):
    return lax.dot_general(a, b, NT, preferred_element_type=F32)


def _dot_tn(a, b):
    return lax.dot_general(a, b, TN, preferred_element_type=F32)


def _split3(x):
    hi = x.astype(BF16)
    r1 = x - hi.astype(F32)
    mid = r1.astype(BF16)
    lo = (r1 - mid.astype(F32)).astype(BF16)
    return hi, mid, lo


def _dot_sel_l(sel, x):
    hi, mid, lo = _split3(x)
    return _dot(sel, hi) + _dot(sel, mid) + _dot(sel, lo)


def _dot_sel_r(x, sel):
    hi, mid, lo = _split3(x)
    return _dot(hi, sel) + _dot(mid, sel) + _dot(lo, sel)


def _dot3(a, b):
    ah, am, _ = _split3(a)
    bh, bm, _ = _split3(b)
    return _dot(ah, bh) + (_dot(ah, bm) + _dot(am, bh))


def _softplus(x):
    return jnp.maximum(x, 0.0) + jnp.log1p(jnp.exp(-jnp.abs(x)))


def _sigmoid(x):
    return 1.0 / (1.0 + jnp.exp(-x))


def _silu(x):
    return x * _sigmoid(x)


def _head_ones():
    r = lax.broadcasted_iota(jnp.int32, (GROUP, GROUP), 0) // HEAD_DIM
    c = lax.broadcasted_iota(jnp.int32, (GROUP, GROUP), 1) // HEAD_DIM
    return jnp.where(r == c, 1.0, 0.0).astype(BF16)


def _head_sum(x):
    return _dot_sel_r(x, _head_ones())


def _params(*sem):
    return pltpu.CompilerParams(dimension_semantics=sem, vmem_limit_bytes=VMEM_LIMIT)


def _mod_kernel(c_ref, w_ref, b_ref, o_ref):
    o_ref[...] = _dot3(_silu(c_ref[...]), w_ref[...]) + b_ref[...]


def _modulation(c_rows, w, b):
    m, d = c_rows.shape
    n = w.shape[1]
    tn = 1536
    return pl.pallas_call(
        _mod_kernel,
        grid=(n // tn,),
        in_specs=[pl.BlockSpec((m, d), lambda j: (0, 0)),
                  pl.BlockSpec((d, tn), lambda j: (0, j)),
                  pl.BlockSpec((1, tn), lambda j: (0, j))],
        out_specs=pl.BlockSpec((m, tn), lambda j: (0, j)),
        out_shape=jax.ShapeDtypeStruct((m, n), F32),
        compiler_params=_params("parallel"),
        name="modulation",
    )(c_rows, w, b.reshape(1, n))


def _modulated_norm(x, g, shift, scale):
    y = x * lax.rsqrt(jnp.mean(x * x, axis=-1, keepdims=True) + RMS_EPS)
    return y * g * (1.0 + scale) + shift


def _inproj_kernel(x_ref, g_ref, sh_ref, sc_ref, w_ref, oa_ref, ob_ref, oc_ref, od_ref):
    h = _modulated_norm(x_ref[...], g_ref[...], sh_ref[...], sc_ref[...]).astype(BF16)
    y = _dot(h, w_ref[...])
    oa_ref[...] = y[:, :WA]
    ob_ref[...] = y[:, WA:WA + WB]
    oc_ref[...] = y[:, WA + WB:WA + WB + WC]
    od_ref[...] = y[:, WA + WB + WC:]


def _in_projection(x, g, shift, scale, w_pad):
    bsz, seq, d = x.shape
    tm = 256
    vec = pl.BlockSpec((None, 1, d), lambda b, i: (b, 0, 0))
    out = lambda w: pl.BlockSpec((None, tm, w), lambda b, i: (b, i, 0))
    shp = lambda w: jax.ShapeDtypeStruct((bsz, seq, w), F32)
    return pl.pallas_call(
        _inproj_kernel,
        grid=(bsz, seq // tm),
        in_specs=[pl.BlockSpec((None, tm, d), lambda b, i: (b, i, 0)),
                  pl.BlockSpec((1, d), lambda b, i: (0, 0)),
                  vec, vec,
                  pl.BlockSpec((d, W_IN_PAD), lambda b, i: (0, 0))],
        out_specs=[out(WA), out(WB), out(WC), out(WD)],
        out_shape=[shp(WA), shp(WB), shp(WC), shp(WD)],
        compiler_params=_params("parallel", "parallel"),
        name="in_projection",
    )(x, g.reshape(1, d), shift, scale, w_pad)


def _conv_kernel(prev_ref, cur_ref, next_ref, w_ref, b_ref, o_ref, ext_ref, *, cols, taps_r, n_tiles):
    i = pl.program_id(1)
    tm, ch = cur_ref.shape
    ext_ref[CONV_HALO:CONV_HALO + tm, :] = cur_ref[...]
    ext_ref[:CONV_HALO, :] = jnp.where(i > 0, prev_ref[...], 0.0)
    ext_ref[CONV_HALO + tm:, :] = jnp.where(i < n_tiles - 1, next_ref[...], 0.0)
    rt = 128
    for r0 in range(0, tm, rt):
        col = (lax.broadcasted_iota(jnp.int32, (rt, LANE), 0) + r0) % cols
        for c0 in range(0, ch, LANE):
            acc = jnp.zeros((rt, LANE), F32) + b_ref[:, c0:c0 + LANE]
            for dr in taps_r:
                for dc in (-1, 0, 1):
                    off = CONV_HALO + r0 + dr * cols + dc
                    tap = ext_ref[off:off + rt, c0:c0 + LANE]
                    if dc != 0:
                        tap = jnp.where((col + dc >= 0) & (col + dc < cols), tap, 0.0)
                    k = (dr + 1) * 3 + (dc + 1)
                    acc = acc + tap * w_ref[k:k + 1, c0:c0 + LANE]
            o_ref[r0:r0 + rt, c0:c0 + LANE] = _silu(acc)


def _grid_conv(proj, lane0, w, b, rows, cols):
    bsz, seq, _ = proj.shape
    tm = min(seq, 512)
    n_tiles = seq // tm
    hpt = tm // CONV_HALO
    n_halo = seq // CONV_HALO
    reach = cols + 1 if rows > 1 else 1
    assert seq == rows * cols and tm % cols == 0 and reach <= CONV_HALO and lane0 % GROUP == 0
    kern = functools.partial(_conv_kernel, cols=cols, taps_r=(-1, 0, 1) if rows > 1 else (0,),
                             n_tiles=n_tiles)
    w9 = w.reshape(9, 3 * GROUP)
    b1 = b.reshape(1, 3 * GROUP)
    outs = []
    for j in range(3):
        lb = lane0 // GROUP + j
        outs.append(pl.pallas_call(
            kern,
            grid=(bsz, n_tiles),
            in_specs=[
                pl.BlockSpec((None, CONV_HALO, GROUP),
                             lambda bi, i, lb=lb: (bi, jnp.maximum(i * hpt - 1, 0), lb)),
                pl.BlockSpec((None, tm, GROUP), lambda bi, i, lb=lb: (bi, i, lb)),
                pl.BlockSpec((None, CONV_HALO, GROUP),
                             lambda bi, i, lb=lb: (bi, jnp.minimum((i + 1) * hpt, n_halo - 1), lb)),
                pl.BlockSpec((9, GROUP), lambda bi, i, j=j: (0, j)),
                pl.BlockSpec((1, GROUP), lambda bi, i, j=j: (0, j))],
            out_specs=pl.BlockSpec((None, tm, GROUP), lambda bi, i: (bi, i, 0)),
            out_shape=jax.ShapeDtypeStruct((bsz, seq, GROUP), F32),
            scratch_shapes=[pltpu.VMEM((tm + 2 * CONV_HALO, GROUP), F32)],
            compiler_params=_params("parallel", "parallel"),
            name="grid_conv",
        )(proj, proj, proj, w9, b1))
    return outs


HW = N_HEADS * CHUNK


def _token_masks(direction, width):
    r = lax.broadcasted_iota(jnp.int32, (CHUNK, width), 0)
    c = lax.broadcasted_iota(jnp.int32, (CHUNK, width), 1) % CHUNK
    diff = (r - c) * (1 - 2 * direction)
    return diff >= 0, diff > 0


def _stacked_masks(direction):
    r = lax.broadcasted_iota(jnp.int32, (HW, HW), 0)
    c = lax.broadcasted_iota(jnp.int32, (HW, HW), 1)
    same = (r // CHUNK) == (c // CHUNK)
    diff = (r - c) * (1 - 2 * direction)
    return same, same & (diff >= 0), same & (diff > 0), r == c


def _tile4(x):
    return jnp.concatenate([x] * N_HEADS, axis=0)


def _stack4(x, same):
    return jnp.where(same, _tile4(x), 0.0)


def _unstack4(y):
    return (y[:CHUNK] + y[CHUNK:2 * CHUNK]) + (y[2 * CHUNK:3 * CHUNK] + y[3 * CHUNK:])


def _neumann_stages(ns, eye):
    ts = [eye + n for n in ns]
    ps = [n.astype(BF16) for n in ns]
    for _ in range(5):
        ps = [_dot(p, p).astype(BF16) for p in ps]
        yield
        for j, (t, p) in enumerate(zip(ts, ps)):
            ts[j] = t + _dot(t.astype(BF16), p)
        yield
    ns[:] = ts


def _cum_rows_cols(g, incl_f, incl_t_f):
    cc = _dot_sel_l(incl_f, g)
    ones = jnp.ones((CHUNK, CHUNK), BF16)
    rr = _dot_sel_l(ones, g * jnp.concatenate([incl_t_f] * N_HEADS, axis=1))
    return cc, rr


DIRS = (0, 1)


def _gla_stages(data, w2_ref, b_ref, s_scr, o_refs):
    prep = []
    for d, (qkv_ref, tail_ref) in zip(DIRS, data):
        same, incl4, _, _ = _stacked_masks(d)
        incl_t, _ = _token_masks(d, CHUNK)
        p = qkv_ref[...]
        q = p[:, :GROUP] * HEAD_DIM ** -0.5
        k = p[:, GROUP:2 * GROUP]
        v = p[:, 2 * GROUP:]
        zg = _dot(tail_ref[...], w2_ref[d]) + b_ref[d]
        gk = (jnp.minimum(zg, 0.0) - jnp.log1p(jnp.exp(-jnp.abs(zg)))) / GLA_GATE_NORM
        bc = _dot_sel_l(jnp.where(incl_t, 1.0, 0.0).astype(BF16), gk)
        bt = jnp.sum(gk, axis=0, keepdims=True)
        prep.append(dict(
            same=same, incl4=incl4,
            qt4=_stack4(q * jnp.exp(bc), same).astype(BF16),
            kt=_tile4(k * jnp.exp(-bc)).astype(BF16),
            ks=_tile4(k * jnp.exp(bt - bc)).astype(BF16),
            v4=_stack4(v, same).astype(BF16), dec=jnp.exp(bt)))
        yield
    att = [jnp.where(c['incl4'], _dot_nt(c['qt4'], c['kt']), 0.0).astype(BF16) for c in prep]
    yield
    sts = [s_scr[d] for d in DIRS]
    o = [_dot(m, c['v4']) + _dot_nt(c['qt4'], s.astype(BF16)) for m, c, s in zip(att, prep, sts)]
    yield
    for d, o_ref in zip(DIRS, o_refs):
        c = prep[d]
        o_ref[...] = _unstack4(o[d])
        s_scr[d] = sts[d] * c['dec'] + jnp.where(c['same'], _dot_tn(c['v4'], c['ks']), 0.0)


def _ssd_stages(data, e_ref, dtb_ref, alog_ref, s_scr, o_refs):
    prep = []
    for d, (tail_ref, xs_ref, bm_ref, cm_ref) in zip(DIRS, data):
        same, _, _, _ = _stacked_masks(d)
        incl_t, _ = _token_masks(d, GROUP)
        incl_c, _ = _token_masks(d, CHUNK)
        incl_f = jnp.where(incl_c, 1.0, 0.0)
        dt = _softplus(_dot_sel_r(tail_ref[...], e_ref[d]) + dtb_ref[d])
        a = -jnp.exp(alog_ref[d]) * dt
        x = xs_ref[...] * dt
        cc, rr = _cum_rows_cols(a, incl_f.astype(BF16), incl_f.T)
        at = jnp.sum(a, axis=0, keepdims=True)
        bm = bm_ref[...]
        cm = cm_ref[...]
        by_head = lambda t: jnp.concatenate(
            [t[:, (h // 2) * SSM_STATE:(h // 2 + 1) * SSM_STATE] for h in range(N_HEADS)], axis=0)
        prep.append(dict(
            lm4=_stack4(jnp.exp(jnp.where(incl_t, cc - rr, -jnp.inf)), same),
            bm4=by_head(bm).astype(BF16), cm4=by_head(cm).astype(BF16),
            x4=_stack4(x, same).astype(BF16),
            xd4=_stack4(x * jnp.exp(at - cc), same).astype(BF16),
            e4=_stack4(jnp.exp(cc), same),
            dec4=jnp.concatenate(
                [jnp.broadcast_to(jnp.exp(at[:, h * HEAD_DIM:h * HEAD_DIM + 1]), (CHUNK, SSM_STATE))
                 for h in range(N_HEADS)], axis=0)))
        yield
    cb = [(_dot_nt(c['cm4'], c['bm4']) * c['lm4']).astype(BF16) for c in prep]
    yield
    sts = [s_scr[d] for d in DIRS]
    y = [_dot(m, c['x4']) + _dot_nt(c['cm4'], s.astype(BF16)) * c['e4'] for m, c, s in zip(cb, prep, sts)]
    yield
    for d, o_ref in zip(DIRS, o_refs):
        c = prep[d]
        o_ref[...] = _unstack4(y[d])
        s_scr[d] = sts[d] * c['dec4'] + _dot_tn(c['xd4'], c['bm4'])


def _rwkv_stages(data, w0_ref, ww_ref, a0_ref, wa_ref, kk_ref, ka_ref, s_scr, o_refs):
    prep = []
    for d, (p_ref,) in zip(DIRS, data):
        same, incl4, strict4, diag4 = _stacked_masks(d)
        incl_t, _ = _token_masks(d, CHUNK)
        p = p_ref[...]
        r = p[:, :GROUP]
        k = p[:, GROUP:2 * GROUP]
        v = p[:, 2 * GROUP:3 * GROUP]
        tail = p[:, 3 * GROUP:]
        wr = w0_ref[d] + _dot(jnp.tanh(tail), ww_ref[d])
        lw = -jnp.exp(-_softplus(-wr) - 0.5)
        a = _sigmoid(a0_ref[...] + _dot(tail, wa_ref[...]))
        kk = k * kk_ref[...]
        kk = kk * lax.rsqrt(_head_sum(kk * kk) + 1e-6)
        kmod = k * (1.0 + (a - 1.0) * ka_ref[...])
        bv = kk * a
        pc = _dot_sel_l(jnp.where(incl_t, 1.0, 0.0).astype(BF16), lw)
        pt = jnp.sum(lw, axis=0, keepdims=True)
        e_npc = jnp.exp(-pc)
        e_rest = jnp.exp(pt - pc)
        zr = jnp.concatenate([_stack4(-kk * jnp.exp(pc - lw), same), _stack4(r * jnp.exp(pc), same)],
                             axis=0).astype(BF16)
        prep.append(dict(
            same=same, incl4=incl4, strict4=strict4, zr=zr,
            bt=_tile4(bv * e_npc).astype(BF16), kt=_tile4(kmod * e_npc).astype(BF16),
            v4=_stack4(v, same).astype(BF16),
            uvb=jnp.concatenate([_tile4(bv * e_rest), _tile4(kmod * e_rest)], axis=0).astype(BF16),
            dec=jnp.exp(pt), eye=jnp.where(diag4, 1.0, 0.0)))
        yield
    ab = [_dot_nt(c['zr'], c['bt']) for c in prep]
    ak = [_dot_nt(c['zr'], c['kt']) for c in prep]
    yield
    ts = [jnp.where(c['strict4'], m[:HW], 0.0) for c, m in zip(prep, ab)]
    a_zk = [jnp.where(c['strict4'], m[:HW], 0.0).astype(BF16) for c, m in zip(prep, ak)]
    a_rb = [jnp.where(c['incl4'], m[HW:], 0.0).astype(BF16) for c, m in zip(prep, ab)]
    a_rk = [jnp.where(c['incl4'], m[HW:], 0.0).astype(BF16) for c, m in zip(prep, ak)]
    zkv = [_dot(m, c['v4']) for m, c in zip(a_zk, prep)]
    yield from _neumann_stages(ts, prep[0]['eye'])
    ts = [t.astype(BF16) for t in ts]
    sts = [s_scr[d] for d in DIRS]
    stb = [s.astype(BF16) for s in sts]
    wu = [_dot(t, c['zr'][:HW]) for t, c in zip(ts, prep)]
    uv = [_dot(t, m.astype(BF16)) for t, m in zip(ts, zkv)]
    yield
    u = [_dot_nt(w.astype(BF16), s) + x for w, s, x in zip(wu, stb, uv)]
    ub = [x.astype(BF16) for x in u]
    yield
    y = [_dot_nt(c['zr'][HW:], s) + _dot(m1, x) + _dot(m2, c['v4'])
         for c, s, m1, x, m2 in zip(prep, stb, a_rb, ub, a_rk)]
    yield
    for d, o_ref in zip(DIRS, o_refs):
        c = prep[d]
        o_ref[...] = _unstack4(y[d])
        upd = _dot_tn(jnp.concatenate([ub[d], c['v4']], axis=0), c['uvb'])
        s_scr[d] = sts[d] * c['dec'] + jnp.where(c['same'], upd, 0.0)


def _gdn_stages(data, eb_ref, ea_ref, dtb_ref, alog_ref, s_scr, o_refs):
    prep = []
    for d, (tail_ref, q_ref, k_ref, v_ref) in zip(DIRS, data):
        same, _, _, diag4 = _stacked_masks(d)
        incl_t, strict_t = _token_masks(d, GROUP)
        incl_c, _ = _token_masks(d, CHUNK)
        incl_f = jnp.where(incl_c, 1.0, 0.0)
        q = q_ref[...]
        q = q * lax.rsqrt(_head_sum(q * q) + 1e-6) * HEAD_DIM ** -0.5
        k = k_ref[...]
        k = k * lax.rsqrt(_head_sum(k * k) + 1e-6)
        v = v_ref[...]
        tail = tail_ref[...]
        beta = _sigmoid(_dot_sel_r(tail, eb_ref[d]))
        g = -jnp.exp(alog_ref[d]) * _softplus(_dot_sel_r(tail, ea_ref[d]) + dtb_ref[d])
        cc, rr = _cum_rows_cols(g, incl_f.astype(BF16), incl_f.T)
        seg = cc - rr
        gt = jnp.sum(g, axis=0, keepdims=True)
        e_cc = jnp.exp(cc)
        kb = k * beta
        prep.append(dict(
            same=same,
            d_strict=_stack4(jnp.exp(jnp.where(strict_t, seg, -jnp.inf)), same),
            d_incl=_stack4(jnp.exp(jnp.where(incl_t, seg, -jnp.inf)), same),
            kb4=_stack4(kb, same).astype(BF16), kt=_tile4(k).astype(BF16),
            q4=_stack4(q, same).astype(BF16),
            rhs=jnp.concatenate([_stack4(kb * e_cc, same), _stack4(v * beta, same)], axis=1).astype(BF16),
            qd4=_stack4(q * e_cc, same).astype(BF16),
            kd4=_stack4(k * jnp.exp(gt - cc), same).astype(BF16),
            dec=jnp.exp(gt), eye=jnp.where(diag4, 1.0, 0.0)))
        yield
    ts = [-(_dot_nt(c['kb4'], c['kt']) * c['d_strict']) for c in prep]
    a_qk = [(_dot_nt(c['q4'], c['kt']) * c['d_incl']).astype(BF16) for c in prep]
    yield
    yield from _neumann_stages(ts, prep[0]['eye'])
    sol = [_dot(t.astype(BF16), c['rhs']) for t, c in zip(ts, prep)]
    yield
    sts = [s_scr[d] for d in DIRS]
    stb = [s.astype(BF16) for s in sts]
    v_new = [(x[:, HW:] - _dot(x[:, :HW].astype(BF16), s)).astype(BF16) for x, s in zip(sol, stb)]
    yield
    o = [_dot(c['qd4'], s) + _dot(m, vn) for c, s, m, vn in zip(prep, stb, a_qk, v_new)]
    yield
    for d, o_ref in zip(DIRS, o_refs):
        c = prep[d]
        o_ref[...] = _unstack4(o[d])
        s_scr[d] = sts[d] * c['dec'] + _dot_tn(c['kd4'], v_new[d])


N_SCAN_DATA = 11
N_SCAN_PARAMS = 15


def _scan_kernel(*refs):
    fwd = refs[:N_SCAN_DATA]
    bwd = refs[N_SCAN_DATA:2 * N_SCAN_DATA]
    params = refs[2 * N_SCAN_DATA:2 * N_SCAN_DATA + N_SCAN_PARAMS]
    rest = refs[2 * N_SCAN_DATA + N_SCAN_PARAMS:]
    inits, outs, finals, scr = rest[:4], rest[4:12], rest[12:16], rest[16:20]
    first = pl.program_id(1) == 0
    last = pl.program_id(1) == pl.num_programs(1) - 1

    @pl.when(first)
    def _():
        for s0_ref, s_scr in zip(inits, scr):
            s_scr[...] = s0_ref[...]

    both = lambda lo, hi: (fwd[lo:hi], bwd[lo:hi])
    (gla_w2, gla_b, rwkv_w0, rwkv_ww, rwkv_a0, rwkv_wa, rwkv_kk, rwkv_ka,
     ssd_e, ssd_dtb, ssd_alog, gdn_eb, gdn_ea, gdn_dtb, gdn_alog) = params
    stages = [
        _gdn_stages(both(7, 11), gdn_eb, gdn_ea, gdn_dtb, gdn_alog, scr[3], outs[6:8]),
        _rwkv_stages(both(2, 3), rwkv_w0, rwkv_ww, rwkv_a0, rwkv_wa, rwkv_kk, rwkv_ka, scr[1], outs[2:4]),
        _gla_stages(both(0, 2), gla_w2, gla_b, scr[0], outs[0:2]),
        _ssd_stages(both(3, 7), ssd_e, ssd_dtb, ssd_alog, scr[2], outs[4:6]),
    ]
    while stages:
        for gen in list(stages):
            if next(gen, stages) is stages:
                stages.remove(gen)

    @pl.when(last)
    def _():
        for sfin_ref, s_scr in zip(finals, scr):
            sfin_ref[...] = s_scr[...]


def _token_mixers(proj, convs, mp, init):
    pa, pb, pc, pd = proj
    (xs, bm, cm), (dq, dk, dv) = convs
    tail_blk = 4 * GROUP // LANE
    data = [(pa, 3 * GROUP, 0), (pa, LANE, tail_blk), (pb, WB, 0),
            (pc, LANE, tail_blk), (xs, GROUP, 0), (bm, GROUP, 0), (cm, GROUP, 0),
            (pd, LANE, tail_blk), (dq, GROUP, 0), (dk, GROUP, 0), (dv, GROUP, 0)]
    params = [mp[k] for k in ('gla_w2', 'gla_b', 'rwkv_w0', 'rwkv_ww', 'rwkv_a0', 'rwkv_wa', 'rwkv_kk',
                              'rwkv_ka', 'ssd_e', 'ssd_dtb', 'ssd_alog', 'gdn_eb', 'gdn_ea', 'gdn_dtb',
                              'gdn_alog')]
    assert len(data) == N_SCAN_DATA and len(params) == N_SCAN_PARAMS
    bsz, seq = pa.shape[:2]
    n_chunks = seq // CHUNK
    in_specs, args = [], []
    for mirror in (False, True):
        for arr, width, blk in data:
            idx = (lambda b, n, blk=blk: (b, n_chunks - 1 - n, blk)) if mirror else \
                  (lambda b, n, blk=blk: (b, n, blk))
            in_specs.append(pl.BlockSpec((None, CHUNK, width), idx))
            args.append(arr)
    for arr in params:
        in_specs.append(pl.BlockSpec(arr.shape, lambda b, n, nd=arr.ndim: (0,) * nd))
        args.append(arr)
    st_specs = [pl.BlockSpec((None,) + s.shape[1:], lambda b, n, nd=s.ndim - 1: (b,) + (0,) * nd)
                for s in init]
    in_specs += st_specs
    args += list(init)
    o_shape = jax.ShapeDtypeStruct((bsz, seq, GROUP), F32)
    o_specs = [pl.BlockSpec((None, CHUNK, GROUP), lambda b, n: (b, n, 0)),
               pl.BlockSpec((None, CHUNK, GROUP), lambda b, n: (b, n_chunks - 1 - n, 0))] * 4
    res = pl.pallas_call(
        _scan_kernel,
        grid=(bsz, n_chunks),
        in_specs=in_specs,
        out_specs=o_specs + st_specs,
        out_shape=[o_shape] * 8 + [jax.ShapeDtypeStruct(s.shape, F32) for s in init],
        scratch_shapes=[pltpu.VMEM(s.shape[1:], F32) for s in init],
        compiler_params=_params("parallel", "arbitrary"),
        name="mixer_scans",
    )(*args)
    return tuple(res[:8]), tuple(res[8:])


def _expand_matrix(row0):
    d = jnp.arange(2)[:, None, None]
    r = jnp.arange(LANE)[None, :, None]
    c = jnp.arange(GROUP)[None, None, :]
    return (r == row0 + N_HEADS * d + c // HEAD_DIM).astype(BF16)


def _per_head_lanes(t):
    return jnp.repeat(t.astype(F32), HEAD_DIM, axis=-1)[:, None, :]


def _rows_at(w, row0, total):
    pad = [(0, 0)] * (w.ndim - 2) + [(row0, total - row0 - w.shape[-2]), (0, 0)]
    return jnp.pad(w.astype(F32), pad)


def _mixer_params(lp):
    f = lambda t: t.astype(F32).reshape(1, GROUP)
    return dict(
        gla_w2=jnp.stack([_rows_at(lp['gla_gate_w2'][0], 0, LANE), _rows_at(lp['gla_gate_w2'][1], 16, LANE)]),
        gla_b=lp['gla_gate_b'].astype(F32)[:, None, :],
        rwkv_w0=lp['rwkv_w0'].astype(F32)[:, None, :],
        rwkv_ww=jnp.stack([_rows_at(lp['rwkv_w_w2'][0], 0, GROUP), _rows_at(lp['rwkv_w_w2'][1], 32, GROUP)]),
        rwkv_a0=f(lp['rwkv_a0']),
        rwkv_wa=_rows_at(lp['rwkv_a_w2'], 64, GROUP),
        rwkv_wg=_rows_at(lp['rwkv_g_w2'], 96, GROUP),
        rwkv_kk=f(lp['rwkv_k_k']),
        rwkv_ka=f(lp['rwkv_k_a']),
        ssd_e=_expand_matrix(0),
        ssd_dtb=_per_head_lanes(lp['ssm_dt_bias']),
        ssd_alog=_per_head_lanes(lp['ssm_A_log']),
        gdn_eb=_expand_matrix(0),
        gdn_ea=_expand_matrix(8),
        gdn_dtb=_per_head_lanes(lp['dn_dt_bias']),
        gdn_alog=_per_head_lanes(lp['dn_A_log']),
    )


V_GLA_G, V_A0, V_KA, V_RK, V_LN_G, V_LN_B, V_SSM_D, V_SSM_G, V_DN_G = range(9)


def _group_ones():
    r = lax.broadcasted_iota(jnp.int32, (GROUP, GROUP), 0) // SSM_STATE
    c = lax.broadcasted_iota(jnp.int32, (GROUP, GROUP), 1) // SSM_STATE
    return jnp.where(r == c, 1.0, 0.0).astype(BF16)


def _outproj_kernel(x_ref, ga_ref, oaf_ref, oab_ref, obf_ref, obb_ref, ocf_ref, ocb_ref, odf_ref, odb_ref,
                    ag_ref, pb_ref, cz_ref, xs_ref, dz_ref, vec_ref, wa_ref, wg_ref, wout_ref, o_ref):
    vec = lambda i: vec_ref[i:i + 1, :]
    inv_hd = 1.0 / HEAD_DIM

    o = oaf_ref[...] + oab_ref[...]
    ya = o * lax.rsqrt(_head_sum(o * o) * inv_hd + RMS_EPS) * vec(V_GLA_G) * _silu(ag_ref[...])

    pb = pb_ref[...]
    r = pb[:, :GROUP]
    k = pb[:, GROUP:2 * GROUP]
    v = pb[:, 2 * GROUP:3 * GROUP]
    tail = pb[:, 3 * GROUP:]
    a = _sigmoid(vec(V_A0) + _dot(tail, wa_ref[...]))
    kmod = k * (1.0 + (a - 1.0) * vec(V_KA))
    gate = _dot(_sigmoid(tail), wg_ref[...])
    o = obf_ref[...] + obb_ref[...]
    dev = o - _head_sum(o) * inv_hd
    on = dev * lax.rsqrt(_head_sum(dev * dev) * inv_hd + RWKV_GN_EPS) * vec(V_LN_G) + vec(V_LN_B)
    yb = (on + _head_sum(r * kmod * vec(V_RK)) * v) * gate

    y = (ocf_ref[...] + ocb_ref[...] + vec(V_SSM_D) * xs_ref[...]) * _silu(cz_ref[...])
    yc = y * lax.rsqrt(_dot_sel_r(y * y, _group_ones()) * (1.0 / SSM_STATE) + RMS_EPS) * vec(V_SSM_G)

    o = odf_ref[...] + odb_ref[...]
    yd = o * lax.rsqrt(_head_sum(o * o) * inv_hd + RMS_EPS) * vec(V_DN_G) * _silu(dz_ref[...])

    cat = jnp.concatenate([ya, yb, yc, yd], axis=1).astype(BF16)
    o_ref[...] = x_ref[...] + ga_ref[...] * _dot(cat, wout_ref[...])


def _out_projection(x, gate, outs, proj, xs, vecs, wa, wg, w_out):
    bsz, seq, d = x.shape
    tm = 256
    pa, pb, pc, pd = proj
    col = lambda blk: pl.BlockSpec((None, tm, GROUP), lambda b, i: (b, i, blk))
    full = lambda arr: pl.BlockSpec(arr.shape, lambda b, i: (0, 0))
    return pl.pallas_call(
        _outproj_kernel,
        grid=(bsz, seq // tm),
        in_specs=[pl.BlockSpec((None, tm, d), lambda b, i: (b, i, 0)),
                  pl.BlockSpec((None, 1, d), lambda b, i: (b, 0, 0)),
                  col(0), col(0), col(0), col(0), col(0), col(0), col(0), col(0),
                  col(3),
                  pl.BlockSpec((None, tm, WB), lambda b, i: (b, i, 0)),
                  col(0), col(0), col(3),
                  full(vecs), full(wa), full(wg), full(w_out)],
        out_specs=pl.BlockSpec((None, tm, d), lambda b, i: (b, i, 0)),
        out_shape=jax.ShapeDtypeStruct(x.shape, F32),
        compiler_params=_params("parallel", "parallel"),
        name="out_projection",
    )(x, gate, *outs, pa, pb, pc, xs, pd, vecs, wa, wg, w_out)


def _route_kernel(x_ref, g_ref, sh_ref, sc_ref, rw_ref, rb_ref, h_ref, wd_ref):
    h = _modulated_norm(x_ref[...], g_ref[...], sh_ref[...], sc_ref[...])
    h_ref[...] = h.astype(BF16)
    scores = _sigmoid(_dot3(h, rw_ref[...]))
    sel = scores + rb_ref[...]
    lane = lax.broadcasted_iota(jnp.int32, sel.shape, 1)
    chosen = jnp.zeros(sel.shape, F32)
    for _ in range(TOP_K):
        m = jnp.max(sel, axis=-1, keepdims=True)
        first = jnp.min(jnp.where(sel == m, lane, N_EXPERTS), axis=-1, keepdims=True)
        pick = lane == first
        chosen = jnp.where(pick, 1.0, chosen)
        sel = jnp.where(pick, -jnp.inf, sel)
    w = scores * chosen
    wd_ref[...] = w / jnp.sum(w, axis=-1, keepdims=True) * ROUTED_SCALE


def _route(x, g, shift, scale, router_w, router_bias):
    bsz, seq, d = x.shape
    tm = 256
    vec = pl.BlockSpec((None, 1, d), lambda b, i: (b, 0, 0))
    return pl.pallas_call(
        _route_kernel,
        grid=(bsz, seq // tm),
        in_specs=[pl.BlockSpec((None, tm, d), lambda b, i: (b, i, 0)),
                  pl.BlockSpec((1, d), lambda b, i: (0, 0)),
                  vec, vec,
                  pl.BlockSpec((d, N_EXPERTS), lambda b, i: (0, 0)),
                  pl.BlockSpec((1, N_EXPERTS), lambda b, i: (0, 0))],
        out_specs=[pl.BlockSpec((None, tm, d), lambda b, i: (b, i, 0)),
                   pl.BlockSpec((None, tm, N_EXPERTS), lambda b, i: (b, i, 0))],
        out_shape=[jax.ShapeDtypeStruct((bsz, seq, d), BF16),
                   jax.ShapeDtypeStruct((bsz, seq, N_EXPERTS), F32)],
        compiler_params=_params("parallel", "parallel"),
        name="moe_route",
    )(x, g.reshape(1, d), shift, scale, router_w, router_bias.reshape(1, N_EXPERTS))


def _swiglu_hidden(h, w_gate_up):
    gu = _dot(h, w_gate_up)
    return _silu(gu[:, :EXPERT_DIM]) * gu[:, EXPERT_DIM:]


def _experts_kernel(h_ref, wd_ref, x_ref, gf_ref, wgu_ref, wdn_ref, sgu_ref, sdn_ref, fg_ref,
                    o_ref, acc_ref, *, final_norm):
    grp = pl.program_id(2)
    per_step = wgu_ref.shape[0]
    h = h_ref[...]

    @pl.when(grp == 0)
    def _():
        acc_ref[...] = _dot(_swiglu_hidden(h, sgu_ref[...]).astype(BF16), sdn_ref[...])

    wd = wd_ref[...]
    lane = lax.broadcasted_iota(jnp.int32, wd.shape, 1)
    acts = []
    for j in range(per_step):
        w_e = jnp.sum(jnp.where(lane == grp * per_step + j, wd, 0.0), axis=-1, keepdims=True)
        acts.append((_swiglu_hidden(h, wgu_ref[j]) * w_e).astype(BF16))
    w_down = wdn_ref[...].reshape(per_step * EXPERT_DIM, wdn_ref.shape[-1])
    acc_ref[...] += _dot(jnp.concatenate(acts, axis=1), w_down)

    @pl.when(grp == pl.num_programs(2) - 1)
    def _():
        y = x_ref[...] + gf_ref[...] * acc_ref[...]
        if final_norm:
            y = y * lax.rsqrt(jnp.mean(y * y, axis=-1, keepdims=True) + RMS_EPS) * fg_ref[...]
        o_ref[...] = y


EXPERTS_PER_STEP = 4


def _experts(h, wd, x, gate, w_gate_up, w_down, sh_gate_up, sh_down, final_g, final_norm):
    bsz, seq, d = x.shape
    tm = min(seq, 1024)
    eps = EXPERTS_PER_STEP
    tok = lambda w: pl.BlockSpec((None, tm, w), lambda b, i, e: (b, i, 0))
    return pl.pallas_call(
        functools.partial(_experts_kernel, final_norm=final_norm),
        grid=(bsz, seq // tm, N_EXPERTS // eps),
        in_specs=[tok(d), tok(N_EXPERTS), tok(d),
                  pl.BlockSpec((None, 1, d), lambda b, i, e: (b, 0, 0)),
                  pl.BlockSpec((eps, d, 2 * EXPERT_DIM), lambda b, i, e: (e, 0, 0)),
                  pl.BlockSpec((eps, EXPERT_DIM, d), lambda b, i, e: (e, 0, 0)),
                  pl.BlockSpec((d, 2 * EXPERT_DIM), lambda b, i, e: (0, 0)),
                  pl.BlockSpec((EXPERT_DIM, d), lambda b, i, e: (0, 0)),
                  pl.BlockSpec((1, d), lambda b, i, e: (0, 0))],
        out_specs=tok(d),
        out_shape=jax.ShapeDtypeStruct(x.shape, F32),
        scratch_shapes=[pltpu.VMEM((tm, d), F32)],
        compiler_params=_params("parallel", "parallel", "arbitrary"),
        name="moe_experts",
    )(h, wd, x, gate, w_gate_up, w_down, sh_gate_up, sh_down, final_g.reshape(1, d))


def _pad_w_in(w):
    cuts = (0, 1056, 1984, 3016, 4056)
    widths = (WA, WB, WC, WD)
    parts = []
    for lo, hi, wd in zip(cuts[:-1], cuts[1:], widths):
        parts.append(jnp.pad(w[:, lo:hi], ((0, 0), (0, wd - (hi - lo)))))
    return jnp.concatenate(parts, axis=1).astype(BF16)


def kernel(x, c, ctx, c_ctx, norm1_g, norm2_g, w_mod, b_mod, w_in, w_out, gla_gate_w2, gla_gate_b, gla_norm_g, rwkv_w_w2, rwkv_w0, rwkv_a_w2, rwkv_a0, rwkv_g_w2, rwkv_k_k, rwkv_k_a, rwkv_r_k, rwkv_ln_g, rwkv_ln_b, ssm_conv_w, ssm_conv_b, ssm_A_log, ssm_dt_bias, ssm_D, ssm_norm_g, dn_conv_w, dn_A_log, dn_dt_bias, dn_norm_g, router_w, router_bias, exp_w_gate, exp_w_up, exp_w_down, sh_w_gate, sh_w_up, sh_w_down, final_norm_g):
    bsz, n_lat, d = x.shape
    n_ctx = ctx.shape[1]
    depth = w_in.shape[0]
    rows = n_lat // GRID_W
    x = x.astype(F32)
    ctx = ctx.astype(F32)
    c_rows = jnp.concatenate([c, c_ctx[None, :], jnp.zeros((8 - bsz - 1, d), c.dtype)], axis=0).astype(F32)
    zero_init = (jnp.zeros((bsz, 2, HW, HW), F32), jnp.zeros((bsz, 2, HW, HW), F32),
                 jnp.zeros((bsz, 2, HW, SSM_STATE), F32), jnp.zeros((bsz, 2, HW, HW), F32))
    for i in range(depth):
        last = i == depth - 1
        lp = dict(gla_gate_w2=gla_gate_w2[i], gla_gate_b=gla_gate_b[i], rwkv_w_w2=rwkv_w_w2[i],
                  rwkv_w0=rwkv_w0[i], rwkv_a_w2=rwkv_a_w2[i], rwkv_a0=rwkv_a0[i],
                  rwkv_g_w2=rwkv_g_w2[i], rwkv_k_k=rwkv_k_k[i], rwkv_k_a=rwkv_k_a[i],
                  ssm_A_log=ssm_A_log[i], ssm_dt_bias=ssm_dt_bias[i],
                  dn_A_log=dn_A_log[i], dn_dt_bias=dn_dt_bias[i])
        mp = _mixer_params(lp)
        vecs = jnp.stack([gla_norm_g[i], rwkv_a0[i], rwkv_k_a[i], rwkv_r_k[i].reshape(GROUP),
                          rwkv_ln_g[i], rwkv_ln_b[i], jnp.repeat(ssm_D[i], HEAD_DIM), ssm_norm_g[i],
                          dn_norm_g[i]] + [jnp.zeros((GROUP,), F32)] * 7).astype(F32)
        w_in_pad = _pad_w_in(w_in[i])
        w_out_b = w_out[i].astype(BF16)
        w_gate_up = jnp.concatenate([exp_w_gate[i], exp_w_up[i]], axis=-1).astype(BF16)
        w_down = exp_w_down[i].astype(BF16)
        sh_gate_up = jnp.concatenate([sh_w_gate[i], sh_w_up[i]], axis=-1).astype(BF16)
        sh_down = sh_w_down[i].astype(BF16)
        dn_conv_b = jnp.zeros((3 * GROUP,), F32)

        mod = _modulation(c_rows, w_mod[i].astype(F32), b_mod[i].astype(F32))
        mod_x = [m[:, None, :] for m in jnp.split(mod[:bsz], 6, axis=-1)]
        mod_c = [jnp.broadcast_to(m[None], (bsz, 1, d)) for m in jnp.split(mod[bsz:bsz + 1], 6, axis=-1)]

        def mix(tokens, mods, init, g_rows, g_cols):
            proj = _in_projection(tokens, norm1_g[i], mods[0], mods[1], w_in_pad)
            convs = (_grid_conv(proj[2], GROUP, ssm_conv_w[i], ssm_conv_b[i], g_rows, g_cols),
                     _grid_conv(proj[3], 0, dn_conv_w[i], dn_conv_b, g_rows, g_cols))
            outs, states = _token_mixers(proj, convs, mp, init)
            return proj, convs, outs, states

        proj_c, convs_c, outs_c, ctx_states = mix(ctx, mod_c, zero_init, 1, n_ctx)
        proj_x, convs_x, outs_x, _ = mix(x, mod_x, ctx_states, rows, GRID_W)
        x = _out_projection(x, mod_x[2], outs_x, proj_x, convs_x[0][0], vecs,
                            mp['rwkv_wa'], mp['rwkv_wg'], w_out_b)

        def ffn(tokens, mods, final, flatten=False):
            shape = tokens.shape
            if flatten:
                tokens = tokens.reshape(1, shape[0] * shape[1], d)
                mods = [m[:1] for m in mods]
            h, wd = _route(tokens, norm2_g[i], mods[3], mods[4], router_w[i].astype(F32),
                           router_bias[i].astype(F32))
            out = _experts(h, wd, tokens, mods[5], w_gate_up, w_down, sh_gate_up, sh_down,
                           final_norm_g.astype(F32), final)
            return out.reshape(shape)

        x = ffn(x, mod_x, last)
        if not last:
            ctx = _out_projection(ctx, mod_c[2], outs_c, proj_c, convs_c[0][0], vecs,
                                  mp['rwkv_wa'], mp['rwkv_wg'], w_out_b)
            ctx = ffn(ctx, mod_c, False, flatten=True)
    return x
```

```python
import functools

import jax
import jax.numpy as jnp
from jax import lax
from jax.experimental import pallas as pl
from jax.experimental.pallas import tpu as pltpu

F32 = jnp.float32
BF16 = jnp.bfloat16

D_MODEL = 1024
GROUP = 256
N_HEADS = 4
HEAD_DIM = 64
CHUNK = 64
GRID_W = 64
SSM_STATE = 128
N_EXPERTS = 64
TOP_K = 8
EXPERT_DIM = 256
ROUTED_SCALE = 2.5
RMS_EPS = 1e-6
RWKV_GN_EPS = 64e-5
GLA_GATE_NORM = 16.0
LANE = 128
CONV_HALO = 128
VMEM_LIMIT = 56 * 1024 * 1024

WA = 4 * GROUP + LANE
WB = 4 * GROUP
WC = 4 * GROUP + LANE
WD = 4 * GROUP + LANE
W_IN_PAD = WA + WB + WC + WD

NT = (((1,), (1,)), ((), ()))
TN = (((0,), (0,)), ((), ()))


def _dot(a, b):
    return jnp.dot(a, b, preferred_element_type=F32)


def _dot_nt(a, b):
    return lax.dot_general(a, b, NT, preferred_element_type=F32)


def _dot_tn(a, b):
    return lax.dot_general(a, b, TN, preferred_element_type=F32)


def _split3(x):
    hi = x.astype(BF16)
    r1 = x - hi.astype(F32)
    mid = r1.astype(BF16)
    lo = (r1 - mid.astype(F32)).astype(BF16)
    return hi, mid, lo


def _dot_sel_l(sel, x):
    hi, mid, lo = _split3(x)
    return _dot(sel, hi) + _dot(sel, mid) + _dot(sel, lo)


def _dot_sel_r(x, sel):
    hi, mid, lo = _split3(x)
    return _dot(hi, sel) + _dot(mid, sel) + _dot(lo, sel)


def _dot3(a, b):
    ah, am, _ = _split3(a)
    bh, bm, _ = _split3(b)
    return _dot(ah, bh) + (_dot(ah, bm) + _dot(am, bh))


def _softplus(x):
    return jnp.maximum(x, 0.0) + jnp.log1p(jnp.exp(-jnp.abs(x)))


def _sigmoid(x):
    return 1.0 / (1.0 + jnp.exp(-x))


def _silu(x):
    return x * _sigmoid(x)


def _head_ones():
    r = lax.broadcasted_iota(jnp.int32, (GROUP, GROUP), 0) // HEAD_DIM
    c = lax.broadcasted_iota(jnp.int32, (GROUP, GROUP), 1) // HEAD_DIM
    return jnp.where(r == c, 1.0, 0.0).astype(BF16)


def _dot_stat(x, sel):
    hi = x.astype(BF16)
    lo = (x - hi.astype(F32)).astype(BF16)
    return _dot(hi, sel) + _dot(lo, sel)


def _head_sum(x):
    return _dot_sel_r(x, _head_ones())


def _head_stat(x):
    return _dot_stat(x, _head_ones())


def _params(*sem):
    return pltpu.CompilerParams(dimension_semantics=sem, vmem_limit_bytes=VMEM_LIMIT)


def _mod_kernel(c_ref, w_ref, b_ref, o_ref):
    o_ref[...] = _dot3(_silu(c_ref[...]), w_ref[...]) + b_ref[...]


def _modulation(c_rows, w, b):
    m, d = c_rows.shape
    n = w.shape[1]
    tn = 1536
    return pl.pallas_call(
        _mod_kernel,
        grid=(n // tn,),
        in_specs=[pl.BlockSpec((m, d), lambda j: (0, 0)),
                  pl.BlockSpec((d, tn), lambda j: (0, j)),
                  pl.BlockSpec((1, tn), lambda j: (0, j))],
        out_specs=pl.BlockSpec((m, tn), lambda j: (0, j)),
        out_shape=jax.ShapeDtypeStruct((m, n), F32),
        compiler_params=_params("parallel"),
        name="modulation",
    )(c_rows, w, b.reshape(1, n))


def _modulated_norm(x, g, shift, scale):
    y = x * lax.rsqrt(jnp.mean(x * x, axis=-1, keepdims=True) + RMS_EPS)
    return y * g * (1.0 + scale) + shift


def _inproj_kernel(x_ref, g_ref, sh_ref, sc_ref, w_ref, oa_ref, ob_ref, oc_ref, od_ref):
    h = _modulated_norm(x_ref[...], g_ref[...], sh_ref[...], sc_ref[...]).astype(BF16)
    y = _dot(h, w_ref[...])
    oa_ref[...] = y[:, :WA]
    ob_ref[...] = y[:, WA:WA + WB]
    oc_ref[...] = y[:, WA + WB:WA + WB + WC]
    od_ref[...] = y[:, WA + WB + WC:]


def _in_projection(x, g, shift, scale, w_pad):
    bsz, seq, d = x.shape
    tm = 256
    vec = pl.BlockSpec((None, 1, d), lambda b, i: (b, 0, 0))
    out = lambda w: pl.BlockSpec((None, tm, w), lambda b, i: (b, i, 0))
    shp = lambda w: jax.ShapeDtypeStruct((bsz, seq, w), F32)
    return pl.pallas_call(
        _inproj_kernel,
        grid=(bsz, seq // tm),
        in_specs=[pl.BlockSpec((None, tm, d), lambda b, i: (b, i, 0)),
                  pl.BlockSpec((1, d), lambda b, i: (0, 0)),
                  vec, vec,
                  pl.BlockSpec((d, W_IN_PAD), lambda b, i: (0, 0))],
        out_specs=[out(WA), out(WB), out(WC), out(WD)],
        out_shape=[shp(WA), shp(WB), shp(WC), shp(WD)],
        compiler_params=_params("parallel", "parallel"),
        name="in_projection",
    )(x, g.reshape(1, d), shift, scale, w_pad)


def _conv_kernel(prev_ref, cur_ref, next_ref, w_ref, b_ref, o_ref, ext_ref, *, cols, taps_r, n_tiles):
    i = pl.program_id(1)
    tm, ch = cur_ref.shape
    ext_ref[CONV_HALO:CONV_HALO + tm, :] = cur_ref[...]
    ext_ref[:CONV_HALO, :] = jnp.where(i > 0, prev_ref[...], 0.0)
    ext_ref[CONV_HALO + tm:, :] = jnp.where(i < n_tiles - 1, next_ref[...], 0.0)
    rt = 128
    for r0 in range(0, tm, rt):
        col = (lax.broadcasted_iota(jnp.int32, (rt, LANE), 0) + r0) % cols
        inside = {-1: col >= 1, 1: col < cols - 1}
        for c0 in range(0, ch, LANE):
            acc = jnp.zeros((rt, LANE), F32) + b_ref[:, c0:c0 + LANE]
            for dc in (-1, 0, 1):
                part = jnp.zeros((rt, LANE), F32)
                for dr in taps_r:
                    off = CONV_HALO + r0 + dr * cols + dc
                    k = (dr + 1) * 3 + (dc + 1)
                    part = part + ext_ref[off:off + rt, c0:c0 + LANE] * w_ref[k:k + 1, c0:c0 + LANE]
                acc = acc + (part if dc == 0 else jnp.where(inside[dc], part, 0.0))
            o_ref[r0:r0 + rt, c0:c0 + LANE] = _silu(acc)


def _grid_conv(proj, lane0, w, b, rows, cols):
    bsz, seq, _ = proj.shape
    tm = min(seq, 512)
    n_tiles = seq // tm
    hpt = tm // CONV_HALO
    n_halo = seq // CONV_HALO
    reach = cols + 1 if rows > 1 else 1
    assert seq == rows * cols and tm % cols == 0 and reach <= CONV_HALO and lane0 % GROUP == 0
    kern = functools.partial(_conv_kernel, cols=cols, taps_r=(-1, 0, 1) if rows > 1 else (0,),
                             n_tiles=n_tiles)
    w9 = w.reshape(9, 3 * GROUP)
    b1 = b.reshape(1, 3 * GROUP)
    outs = []
    for j in range(3):
        lb = lane0 // GROUP + j
        outs.append(pl.pallas_call(
            kern,
            grid=(bsz, n_tiles),
            in_specs=[
                pl.BlockSpec((None, CONV_HALO, GROUP),
                             lambda bi, i, lb=lb: (bi, jnp.maximum(i * hpt - 1, 0), lb)),
                pl.BlockSpec((None, tm, GROUP), lambda bi, i, lb=lb: (bi, i, lb)),
                pl.BlockSpec((None, CONV_HALO, GROUP),
                             lambda bi, i, lb=lb: (bi, jnp.minimum((i + 1) * hpt, n_halo - 1), lb)),
                pl.BlockSpec((9, GROUP), lambda bi, i, j=j: (0, j)),
                pl.BlockSpec((1, GROUP), lambda bi, i, j=j: (0, j))],
            out_specs=pl.BlockSpec((None, tm, GROUP), lambda bi, i: (bi, i, 0)),
            out_shape=jax.ShapeDtypeStruct((bsz, seq, GROUP), F32),
            scratch_shapes=[pltpu.VMEM((tm + 2 * CONV_HALO, GROUP), F32)],
            compiler_params=_params("parallel", "parallel"),
            name="grid_conv",
        )(proj, proj, proj, w9, b1))
    return outs


HW = N_HEADS * CHUNK


def _token_masks(direction, width):
    r = lax.broadcasted_iota(jnp.int32, (CHUNK, width), 0)
    c = lax.broadcasted_iota(jnp.int32, (CHUNK, width), 1) % CHUNK
    diff = (r - c) * (1 - 2 * direction)
    return diff >= 0, diff > 0


def _stacked_masks(direction):
    r = lax.broadcasted_iota(jnp.int32, (HW, HW), 0)
    c = lax.broadcasted_iota(jnp.int32, (HW, HW), 1)
    same = (r // CHUNK) == (c // CHUNK)
    diff = (r - c) * (1 - 2 * direction)
    return same, same & (diff >= 0), same & (diff > 0), r == c


def _tile4(x):
    return jnp.concatenate([x] * N_HEADS, axis=0)


def _stack4(x, same):
    return jnp.where(same, _tile4(x), 0.0)


def _unstack4(y):
    return (y[:CHUNK] + y[CHUNK:2 * CHUNK]) + (y[2 * CHUNK:3 * CHUNK] + y[3 * CHUNK:])


def _neumann_stages(ns, eye):
    ts = [eye + n for n in ns]
    ps = [n.astype(BF16) for n in ns]
    for _ in range(5):
        ps = [_dot(p, p).astype(BF16) for p in ps]
        yield
        for j, (t, p) in enumerate(zip(ts, ps)):
            ts[j] = t + _dot(t.astype(BF16), p)
        yield
    ns[:] = ts


def _cum_rows_cols(g, incl_f, incl_t_f):
    cc = _dot_sel_l(incl_f, g)
    ones = jnp.ones((CHUNK, CHUNK), BF16)
    rr = _dot_sel_l(ones, g * jnp.concatenate([incl_t_f] * N_HEADS, axis=1))
    return cc, rr


DIRS = (0, 1)
CHUNKS_PER_STEP = 2
STEP_ROWS = CHUNKS_PER_STEP * CHUNK


def _visit_order(direction):
    order = tuple(range(CHUNKS_PER_STEP))
    return order if direction == 0 else order[::-1]


def _wave(k):
    return [(d, _visit_order(d)[k]) for d in DIRS]


def _rows(ref, c):
    return ref[c * CHUNK:(c + 1) * CHUNK, :]


def _gla_stages(wave, data, w2_ref, b_ref, s_scr, o_refs):
    prep = {}
    for d, c in _wave(wave):
        qkv_ref, tail_ref = data[d]
        same, incl4, _, _ = _stacked_masks(d)
        incl_t, _ = _token_masks(d, CHUNK)
        p = _rows(qkv_ref, c)
        q = p[:, :GROUP] * HEAD_DIM ** -0.5
        k = p[:, GROUP:2 * GROUP]
        v = p[:, 2 * GROUP:]
        zg = _dot(_rows(tail_ref, c), w2_ref[d]) + b_ref[d]
        gk = (jnp.minimum(zg, 0.0) - jnp.log1p(jnp.exp(-jnp.abs(zg)))) / GLA_GATE_NORM
        bc = _dot_sel_l(jnp.where(incl_t, 1.0, 0.0).astype(BF16), gk)
        bt = jnp.sum(gk, axis=0, keepdims=True)
        prep[d, c] = dict(
            same=same, incl4=incl4,
            qt4=_stack4(q * jnp.exp(bc), same).astype(BF16),
            kt=_tile4(k * jnp.exp(-bc)).astype(BF16),
            ks=_tile4(k * jnp.exp(bt - bc)).astype(BF16),
            v4=_stack4(v, same).astype(BF16), dec=jnp.exp(bt))
        yield
    att = {key: jnp.where(p['incl4'], _dot_nt(p['qt4'], p['kt']), 0.0).astype(BF16)
           for key, p in prep.items()}
    yield
    intra = {key: _dot(att[key], p['v4']) for key, p in prep.items()}
    upd = {key: jnp.where(p['same'], _dot_tn(p['v4'], p['ks']), 0.0) for key, p in prep.items()}
    yield
    for (d, c), p in prep.items():
        st = s_scr[d]
        o_refs[d][c * CHUNK:(c + 1) * CHUNK, :] = _unstack4(
            intra[d, c] + _dot_nt(p['qt4'], st.astype(BF16)))
        s_scr[d] = st * p['dec'] + upd[d, c]


def _ssd_stages(wave, data, e_ref, dtb_ref, alog_ref, s_scr, o_refs):
    prep = {}
    for d, c in _wave(wave):
        tail_ref, xs_ref, bm_ref, cm_ref = data[d]
        same, _, _, _ = _stacked_masks(d)
        incl_t, _ = _token_masks(d, GROUP)
        incl_c, _ = _token_masks(d, CHUNK)
        incl_f = jnp.where(incl_c, 1.0, 0.0)
        dt = _softplus(_dot_sel_r(_rows(tail_ref, c), e_ref[d]) + dtb_ref[d])
        a = -jnp.exp(alog_ref[d]) * dt
        x = _rows(xs_ref, c) * dt
        cc, rr = _cum_rows_cols(a, incl_f.astype(BF16), incl_f.T)
        at = jnp.sum(a, axis=0, keepdims=True)
        bm = _rows(bm_ref, c)
        cm = _rows(cm_ref, c)
        by_head = lambda t: jnp.concatenate(
            [t[:, (h // 2) * SSM_STATE:(h // 2 + 1) * SSM_STATE] for h in range(N_HEADS)], axis=0)
        prep[d, c] = dict(
            lm4=_stack4(jnp.exp(jnp.where(incl_t, cc - rr, -jnp.inf)), same),
            bm4=by_head(bm).astype(BF16), cm4=by_head(cm).astype(BF16),
            x4=_stack4(x, same).astype(BF16),
            xd4=_stack4(x * jnp.exp(at - cc), same).astype(BF16),
            e4=_stack4(jnp.exp(cc), same),
            dec4=jnp.concatenate(
                [jnp.broadcast_to(jnp.exp(at[:, h * HEAD_DIM:h * HEAD_DIM + 1]), (CHUNK, SSM_STATE))
                 for h in range(N_HEADS)], axis=0))
        yield
    cb = {key: (_dot_nt(p['cm4'], p['bm4']) * p['lm4']).astype(BF16) for key, p in prep.items()}
    yield
    intra = {key: _dot(cb[key], p['x4']) for key, p in prep.items()}
    upd = {key: _dot_tn(p['xd4'], p['bm4']) for key, p in prep.items()}
    yield
    for (d, c), p in prep.items():
        st = s_scr[d]
        o_refs[d][c * CHUNK:(c + 1) * CHUNK, :] = _unstack4(
            intra[d, c] + _dot_nt(p['cm4'], st.astype(BF16)) * p['e4'])
        s_scr[d] = st * p['dec4'] + upd[d, c]


def _rwkv_stages(wave, data, w0_ref, ww_ref, a0_ref, wa_ref, kk_ref, ka_ref, s_scr, o_refs):
    prep = {}
    for d, c in _wave(wave):
        (p_ref,) = data[d]
        same, incl4, strict4, diag4 = _stacked_masks(d)
        incl_t, _ = _token_masks(d, CHUNK)
        p = _rows(p_ref, c)
        r = p[:, :GROUP]
        k = p[:, GROUP:2 * GROUP]
        v = p[:, 2 * GROUP:3 * GROUP]
        tail = p[:, 3 * GROUP:]
        wr = w0_ref[d] + _dot(jnp.tanh(tail), ww_ref[d])
        lw = -jnp.exp(-_softplus(-wr) - 0.5)
        a = _sigmoid(a0_ref[...] + _dot(tail, wa_ref[...]))
        kk = k * kk_ref[...]
        kk = kk * lax.rsqrt(_head_sum(kk * kk) + 1e-6)
        kmod = k * (1.0 + (a - 1.0) * ka_ref[...])
        bv = kk * a
        pc = _dot_sel_l(jnp.where(incl_t, 1.0, 0.0).astype(BF16), lw)
        pt = jnp.sum(lw, axis=0, keepdims=True)
        e_npc = jnp.exp(-pc)
        e_rest = jnp.exp(pt - pc)
        zr = jnp.concatenate([_stack4(-kk * jnp.exp(pc - lw), same), _stack4(r * jnp.exp(pc), same)],
                             axis=0).astype(BF16)
        prep[d, c] = dict(
            same=same, incl4=incl4, strict4=strict4, zr=zr,
            bt=_tile4(bv * e_npc).astype(BF16), kt=_tile4(kmod * e_npc).astype(BF16),
            v4=_stack4(v, same).astype(BF16),
            uvb=jnp.concatenate([_tile4(bv * e_rest), _tile4(kmod * e_rest)], axis=0).astype(BF16),
            dec=jnp.exp(pt), eye=jnp.where(diag4, 1.0, 0.0))
        yield
    keys = list(prep)
    ab = {key: _dot_nt(p['zr'], p['bt']) for key, p in prep.items()}
    ak = {key: _dot_nt(p['zr'], p['kt']) for key, p in prep.items()}
    yield
    ts = [jnp.where(prep[key]['strict4'], ab[key][:HW], 0.0) for key in keys]
    a_zk = {key: jnp.where(p['strict4'], ak[key][:HW], 0.0).astype(BF16) for key, p in prep.items()}
    a_rb = {key: jnp.where(p['incl4'], ab[key][HW:], 0.0).astype(BF16) for key, p in prep.items()}
    a_rk = {key: jnp.where(p['incl4'], ak[key][HW:], 0.0).astype(BF16) for key, p in prep.items()}
    zkv = {key: _dot(a_zk[key], p['v4']) for key, p in prep.items()}
    rkv = {key: _dot(a_rk[key], p['v4']) for key, p in prep.items()}
    yield from _neumann_stages(ts, prep[keys[0]]['eye'])
    tinv = {key: t.astype(BF16) for key, t in zip(keys, ts)}
    wu = {key: _dot(tinv[key], p['zr'][:HW]).astype(BF16) for key, p in prep.items()}
    uv = {key: _dot(tinv[key], zkv[key].astype(BF16)) for key in keys}
    yield
    st = {d: s_scr[d] for d, _ in keys}
    ub = {}
    for (d, c), p in prep.items():
        stb = st[d].astype(BF16)
        ub[d] = (_dot_nt(wu[d, c], stb) + uv[d, c]).astype(BF16)
        o_refs[d][c * CHUNK:(c + 1) * CHUNK, :] = _unstack4(
            _dot_nt(p['zr'][HW:], stb) + _dot(a_rb[d, c], ub[d]) + rkv[d, c])
    yield
    for (d, c), p in prep.items():
        upd = _dot_tn(jnp.concatenate([ub[d], p['v4']], axis=0), p['uvb'])
        s_scr[d] = st[d] * p['dec'] + jnp.where(p['same'], upd, 0.0)


def _gdn_stages(wave, data, eb_ref, ea_ref, dtb_ref, alog_ref, s_scr, o_refs):
    prep = {}
    for d, c in _wave(wave):
        tail_ref, q_ref, k_ref, v_ref = data[d]
        same, _, _, diag4 = _stacked_masks(d)
        incl_t, strict_t = _token_masks(d, GROUP)
        incl_c, _ = _token_masks(d, CHUNK)
        incl_f = jnp.where(incl_c, 1.0, 0.0)
        q = _rows(q_ref, c)
        q = q * lax.rsqrt(_head_sum(q * q) + 1e-6) * HEAD_DIM ** -0.5
        k = _rows(k_ref, c)
        k = k * lax.rsqrt(_head_sum(k * k) + 1e-6)
        v = _rows(v_ref, c)
        tail = _rows(tail_ref, c)
        beta = _sigmoid(_dot_sel_r(tail, eb_ref[d]))
        g = -jnp.exp(alog_ref[d]) * _softplus(_dot_sel_r(tail, ea_ref[d]) + dtb_ref[d])
        cc, rr = _cum_rows_cols(g, incl_f.astype(BF16), incl_f.T)
        seg = cc - rr
        gt = jnp.sum(g, axis=0, keepdims=True)
        e_cc = jnp.exp(cc)
        kb = k * beta
        prep[d, c] = dict(
            same=same,
            d_strict=_stack4(jnp.exp(jnp.where(strict_t, seg, -jnp.inf)), same),
            d_incl=_stack4(jnp.exp(jnp.where(incl_t, seg, -jnp.inf)), same),
            kb4=_stack4(kb, same).astype(BF16), kt=_tile4(k).astype(BF16),
            q4=_stack4(q, same).astype(BF16),
            rhs=jnp.concatenate([_stack4(kb * e_cc, same), _stack4(v * beta, same)], axis=1).astype(BF16),
            qd4=_stack4(q * e_cc, same).astype(BF16),
            kd4=_stack4(k * jnp.exp(gt - cc), same).astype(BF16),
            dec=jnp.exp(gt), eye=jnp.where(diag4, 1.0, 0.0))
        yield
    keys = list(prep)
    ts = [-(_dot_nt(prep[key]['kb4'], prep[key]['kt']) * prep[key]['d_strict']) for key in keys]
    a_qk = {key: (_dot_nt(p['q4'], p['kt']) * p['d_incl']).astype(BF16) for key, p in prep.items()}
    yield
    yield from _neumann_stages(ts, prep[keys[0]]['eye'])
    sol = {key: _dot(t.astype(BF16), prep[key]['rhs']) for key, t in zip(keys, ts)}
    yield
    st = {d: s_scr[d] for d, _ in keys}
    v_new = {}
    for d, c in keys:
        x = sol[d, c]
        v_new[d] = (x[:, HW:] - _dot(x[:, :HW].astype(BF16), st[d].astype(BF16))).astype(BF16)
    yield
    for (d, c), p in prep.items():
        o_refs[d][c * CHUNK:(c + 1) * CHUNK, :] = _unstack4(
            _dot(p['qd4'], st[d].astype(BF16)) + _dot(a_qk[d, c], v_new[d]))
        s_scr[d] = st[d] * p['dec'] + _dot_tn(p['kd4'], v_new[d])


WAVE_LAG = 3
N_SCAN_DATA = 11
N_SCAN_PARAMS = 15


def _scan_kernel(*refs):
    fwd = refs[:N_SCAN_DATA]
    bwd = refs[N_SCAN_DATA:2 * N_SCAN_DATA]
    params = refs[2 * N_SCAN_DATA:2 * N_SCAN_DATA + N_SCAN_PARAMS]
    rest = refs[2 * N_SCAN_DATA + N_SCAN_PARAMS:]
    inits, outs, finals, scr = rest[:4], rest[4:12], rest[12:16], rest[16:20]
    first = pl.program_id(1) == 0
    last = pl.program_id(1) == pl.num_programs(1) - 1

    @pl.when(first)
    def _():
        for s0_ref, s_scr in zip(inits, scr):
            s_scr[...] = s0_ref[...]

    both = lambda lo, hi: (fwd[lo:hi], bwd[lo:hi])
    (gla_w2, gla_b, rwkv_w0, rwkv_ww, rwkv_a0, rwkv_wa, rwkv_kk, rwkv_ka,
     ssd_e, ssd_dtb, ssd_alog, gdn_eb, gdn_ea, gdn_dtb, gdn_alog) = params
    def mixers(wave):
        return [
            _gdn_stages(wave, both(7, 11), gdn_eb, gdn_ea, gdn_dtb, gdn_alog, scr[3], outs[6:8]),
            _rwkv_stages(wave, both(2, 3), rwkv_w0, rwkv_ww, rwkv_a0, rwkv_wa, rwkv_kk, rwkv_ka, scr[1],
                         outs[2:4]),
            _gla_stages(wave, both(0, 2), gla_w2, gla_b, scr[0], outs[0:2]),
            _ssd_stages(wave, both(3, 7), ssd_e, ssd_dtb, ssd_alog, scr[2], outs[4:6]),
        ]

    live = [(k * WAVE_LAG, gen) for k in range(CHUNKS_PER_STEP) for gen in mixers(k)]
    rnd = 0
    while live:
        for entry in list(live):
            if rnd >= entry[0] and next(entry[1], live) is live:
                live.remove(entry)
        rnd += 1

    @pl.when(last)
    def _():
        for sfin_ref, s_scr in zip(finals, scr):
            sfin_ref[...] = s_scr[...]


def _token_mixers(proj, convs, mp, init):
    pa, pb, pc, pd = proj
    (xs, bm, cm), (dq, dk, dv) = convs
    tail_blk = 4 * GROUP // LANE
    data = [(pa, 3 * GROUP, 0), (pa, LANE, tail_blk), (pb, WB, 0),
            (pc, LANE, tail_blk), (xs, GROUP, 0), (bm, GROUP, 0), (cm, GROUP, 0),
            (pd, LANE, tail_blk), (dq, GROUP, 0), (dk, GROUP, 0), (dv, GROUP, 0)]
    params = [mp[k] for k in ('gla_w2', 'gla_b', 'rwkv_w0', 'rwkv_ww', 'rwkv_a0', 'rwkv_wa', 'rwkv_kk',
                              'rwkv_ka', 'ssd_e', 'ssd_dtb', 'ssd_alog', 'gdn_eb', 'gdn_ea', 'gdn_dtb',
                              'gdn_alog')]
    assert len(data) == N_SCAN_DATA and len(params) == N_SCAN_PARAMS
    bsz, seq = pa.shape[:2]
    n_steps = seq // STEP_ROWS
    in_specs, args = [], []
    for mirror in (False, True):
        for arr, width, blk in data:
            idx = (lambda b, n, blk=blk: (b, n_steps - 1 - n, blk)) if mirror else \
                  (lambda b, n, blk=blk: (b, n, blk))
            in_specs.append(pl.BlockSpec((None, STEP_ROWS, width), idx))
            args.append(arr)
    for arr in params:
        in_specs.append(pl.BlockSpec(arr.shape, lambda b, n, nd=arr.ndim: (0,) * nd))
        args.append(arr)
    st_specs = [pl.BlockSpec((None,) + s.shape[1:], lambda b, n, nd=s.ndim - 1: (b,) + (0,) * nd)
                for s in init]
    in_specs += st_specs
    args += list(init)
    o_shape = jax.ShapeDtypeStruct((bsz, seq, GROUP), F32)
    o_specs = [pl.BlockSpec((None, STEP_ROWS, GROUP), lambda b, n: (b, n, 0)),
               pl.BlockSpec((None, STEP_ROWS, GROUP), lambda b, n: (b, n_steps - 1 - n, 0))] * 4
    res = pl.pallas_call(
        _scan_kernel,
        grid=(bsz, n_steps),
        in_specs=in_specs,
        out_specs=o_specs + st_specs,
        out_shape=[o_shape] * 8 + [jax.ShapeDtypeStruct(s.shape, F32) for s in init],
        scratch_shapes=[pltpu.VMEM(s.shape[1:], F32) for s in init],
        compiler_params=_params("parallel", "arbitrary"),
        name="mixer_scans",
    )(*args)
    return tuple(res[:8]), tuple(res[8:])


def _expand_matrix(row0):
    d = jnp.arange(2)[:, None, None]
    r = jnp.arange(LANE)[None, :, None]
    c = jnp.arange(GROUP)[None, None, :]
    return (r == row0 + N_HEADS * d + c // HEAD_DIM).astype(BF16)


def _per_head_lanes(t):
    return jnp.repeat(t.astype(F32), HEAD_DIM, axis=-1)[:, None, :]


def _rows_at(w, row0, total):
    pad = [(0, 0)] * (w.ndim - 2) + [(row0, total - row0 - w.shape[-2]), (0, 0)]
    return jnp.pad(w.astype(F32), pad)


def _mixer_params(lp):
    f = lambda t: t.astype(F32).reshape(1, GROUP)
    return dict(
        gla_w2=jnp.stack([_rows_at(lp['gla_gate_w2'][0], 0, LANE), _rows_at(lp['gla_gate_w2'][1], 16, LANE)]),
        gla_b=lp['gla_gate_b'].astype(F32)[:, None, :],
        rwkv_w0=lp['rwkv_w0'].astype(F32)[:, None, :],
        rwkv_ww=jnp.stack([_rows_at(lp['rwkv_w_w2'][0], 0, GROUP), _rows_at(lp['rwkv_w_w2'][1], 32, GROUP)]),
        rwkv_a0=f(lp['rwkv_a0']),
        rwkv_wa=_rows_at(lp['rwkv_a_w2'], 64, GROUP),
        rwkv_wg=_rows_at(lp['rwkv_g_w2'], 96, GROUP),
        rwkv_kk=f(lp['rwkv_k_k']),
        rwkv_ka=f(lp['rwkv_k_a']),
        ssd_e=_expand_matrix(0),
        ssd_dtb=_per_head_lanes(lp['ssm_dt_bias']),
        ssd_alog=_per_head_lanes(lp['ssm_A_log']),
        gdn_eb=_expand_matrix(0),
        gdn_ea=_expand_matrix(8),
        gdn_dtb=_per_head_lanes(lp['dn_dt_bias']),
        gdn_alog=_per_head_lanes(lp['dn_A_log']),
    )


V_GLA_G, V_A0, V_KA, V_RK, V_LN_G, V_LN_B, V_SSM_D, V_SSM_G, V_DN_G = range(9)


def _group_ones():
    r = lax.broadcasted_iota(jnp.int32, (GROUP, GROUP), 0) // SSM_STATE
    c = lax.broadcasted_iota(jnp.int32, (GROUP, GROUP), 1) // SSM_STATE
    return jnp.where(r == c, 1.0, 0.0).astype(BF16)


def _outproj_kernel(x_ref, ga_ref, oaf_ref, oab_ref, obf_ref, obb_ref, ocf_ref, ocb_ref, odf_ref, odb_ref,
                    ag_ref, pb_ref, cz_ref, xs_ref, dz_ref, vec_ref, wa_ref, wg_ref, wout_ref, o_ref):
    vec = lambda i: vec_ref[i:i + 1, :]
    inv_hd = 1.0 / HEAD_DIM

    o = oaf_ref[...] + oab_ref[...]
    ya = o * lax.rsqrt(_head_stat(o * o) * inv_hd + RMS_EPS) * vec(V_GLA_G) * _silu(ag_ref[...])

    pb = pb_ref[...]
    r = pb[:, :GROUP]
    k = pb[:, GROUP:2 * GROUP]
    v = pb[:, 2 * GROUP:3 * GROUP]
    tail = pb[:, 3 * GROUP:]
    a = _sigmoid(vec(V_A0) + _dot(tail, wa_ref[...]))
    kmod = k * (1.0 + (a - 1.0) * vec(V_KA))
    gate = _dot(_sigmoid(tail), wg_ref[...])
    o = obf_ref[...] + obb_ref[...]
    dev = o - _head_stat(o) * inv_hd
    on = dev * lax.rsqrt(_head_stat(dev * dev) * inv_hd + RWKV_GN_EPS) * vec(V_LN_G) + vec(V_LN_B)
    yb = (on + _head_stat(r * kmod * vec(V_RK)) * v) * gate

    y = (ocf_ref[...] + ocb_ref[...] + vec(V_SSM_D) * xs_ref[...]) * _silu(cz_ref[...])
    yc = y * lax.rsqrt(_dot_stat(y * y, _group_ones()) * (1.0 / SSM_STATE) + RMS_EPS) * vec(V_SSM_G)

    o = odf_ref[...] + odb_ref[...]
    yd = o * lax.rsqrt(_head_stat(o * o) * inv_hd + RMS_EPS) * vec(V_DN_G) * _silu(dz_ref[...])

    cat = jnp.concatenate([ya, yb, yc, yd], axis=1).astype(BF16)
    o_ref[...] = x_ref[...] + ga_ref[...] * _dot(cat, wout_ref[...])


def _out_projection(x, gate, outs, proj, xs, vecs, wa, wg, w_out):
    bsz, seq, d = x.shape
    tm = 256
    pa, pb, pc, pd = proj
    col = lambda blk: pl.BlockSpec((None, tm, GROUP), lambda b, i: (b, i, blk))
    full = lambda arr: pl.BlockSpec(arr.shape, lambda b, i: (0, 0))
    return pl.pallas_call(
        _outproj_kernel,
        grid=(bsz, seq // tm),
        in_specs=[pl.BlockSpec((None, tm, d), lambda b, i: (b, i, 0)),
                  pl.BlockSpec((None, 1, d), lambda b, i: (b, 0, 0)),
                  col(0), col(0), col(0), col(0), col(0), col(0), col(0), col(0),
                  col(3),
                  pl.BlockSpec((None, tm, WB), lambda b, i: (b, i, 0)),
                  col(0), col(0), col(3),
                  full(vecs), full(wa), full(wg), full(w_out)],
        out_specs=pl.BlockSpec((None, tm, d), lambda b, i: (b, i, 0)),
        out_shape=jax.ShapeDtypeStruct(x.shape, F32),
        compiler_params=_params("parallel", "parallel"),
        name="out_projection",
    )(x, gate, *outs, pa, pb, pc, xs, pd, vecs, wa, wg, w_out)


def _route_kernel(x_ref, g_ref, sh_ref, sc_ref, rw_ref, rb_ref, h_ref, wd_ref):
    h = _modulated_norm(x_ref[...], g_ref[...], sh_ref[...], sc_ref[...])
    h_ref[...] = h.astype(BF16)
    scores = _sigmoid(_dot3(h, rw_ref[...]))
    sel = scores + rb_ref[...]
    lane = lax.broadcasted_iota(jnp.int32, sel.shape, 1)
    chosen = jnp.zeros(sel.shape, F32)
    for _ in range(TOP_K):
        m = jnp.max(sel, axis=-1, keepdims=True)
        first = jnp.min(jnp.where(sel == m, lane, N_EXPERTS), axis=-1, keepdims=True)
        pick = lane == first
        chosen = jnp.where(pick, 1.0, chosen)
        sel = jnp.where(pick, -jnp.inf, sel)
    w = scores * chosen
    wd_ref[...] = w / jnp.sum(w, axis=-1, keepdims=True) * ROUTED_SCALE


def _route(x, g, shift, scale, router_w, router_bias):
    bsz, seq, d = x.shape
    tm = 256
    vec = pl.BlockSpec((None, 1, d), lambda b, i: (b, 0, 0))
    return pl.pallas_call(
        _route_kernel,
        grid=(bsz, seq // tm),
        in_specs=[pl.BlockSpec((None, tm, d), lambda b, i: (b, i, 0)),
                  pl.BlockSpec((1, d), lambda b, i: (0, 0)),
                  vec, vec,
                  pl.BlockSpec((d, N_EXPERTS), lambda b, i: (0, 0)),
                  pl.BlockSpec((1, N_EXPERTS), lambda b, i: (0, 0))],
        out_specs=[pl.BlockSpec((None, tm, d), lambda b, i: (b, i, 0)),
                   pl.BlockSpec((None, tm, N_EXPERTS), lambda b, i: (b, i, 0))],
        out_shape=[jax.ShapeDtypeStruct((bsz, seq, d), BF16),
                   jax.ShapeDtypeStruct((bsz, seq, N_EXPERTS), F32)],
        compiler_params=_params("parallel", "parallel"),
        name="moe_route",
    )(x, g.reshape(1, d), shift, scale, router_w, router_bias.reshape(1, N_EXPERTS))


def _swiglu_hidden(h, w_gate_up):
    gu = _dot(h, w_gate_up)
    return _silu(gu[:, :EXPERT_DIM]) * gu[:, EXPERT_DIM:]


def _experts_kernel(h_ref, wd_ref, x_ref, gf_ref, wgu_ref, wdn_ref, sgu_ref, sdn_ref, fg_ref,
                    o_ref, acc_ref, *, final_norm):
    grp = pl.program_id(2)
    per_step = wgu_ref.shape[0]
    h = h_ref[...]

    @pl.when(grp == 0)
    def _():
        acc_ref[...] = _dot(_swiglu_hidden(h, sgu_ref[...]).astype(BF16), sdn_ref[...])

    wd = wd_ref[...]
    lane = lax.broadcasted_iota(jnp.int32, wd.shape, 1)
    acts = []
    for j in range(per_step):
        w_e = jnp.sum(jnp.where(lane == grp * per_step + j, wd, 0.0), axis=-1, keepdims=True)
        acts.append((_swiglu_hidden(h, wgu_ref[j]) * w_e).astype(BF16))
    w_down = wdn_ref[...].reshape(per_step * EXPERT_DIM, wdn_ref.shape[-1])
    acc_ref[...] += _dot(jnp.concatenate(acts, axis=1), w_down)

    @pl.when(grp == pl.num_programs(2) - 1)
    def _():
        y = x_ref[...] + gf_ref[...] * acc_ref[...]
        if final_norm:
            y = y * lax.rsqrt(jnp.mean(y * y, axis=-1, keepdims=True) + RMS_EPS) * fg_ref[...]
        o_ref[...] = y


EXPERTS_PER_STEP = 4


def _experts(h, wd, x, gate, w_gate_up, w_down, sh_gate_up, sh_down, final_g, final_norm):
    bsz, seq, d = x.shape
    tm = min(seq, 1024)
    eps = EXPERTS_PER_STEP
    tok = lambda w: pl.BlockSpec((None, tm, w), lambda b, i, e: (b, i, 0))
    return pl.pallas_call(
        functools.partial(_experts_kernel, final_norm=final_norm),
        grid=(bsz, seq // tm, N_EXPERTS // eps),
        in_specs=[tok(d), tok(N_EXPERTS), tok(d),
                  pl.BlockSpec((None, 1, d), lambda b, i, e: (b, 0, 0)),
                  pl.BlockSpec((eps, d, 2 * EXPERT_DIM), lambda b, i, e: (e, 0, 0)),
                  pl.BlockSpec((eps, EXPERT_DIM, d), lambda b, i, e: (e, 0, 0)),
                  pl.BlockSpec((d, 2 * EXPERT_DIM), lambda b, i, e: (0, 0)),
                  pl.BlockSpec((EXPERT_DIM, d), lambda b, i, e: (0, 0)),
                  pl.BlockSpec((1, d), lambda b, i, e: (0, 0))],
        out_specs=tok(d),
        out_shape=jax.ShapeDtypeStruct(x.shape, F32),
        scratch_shapes=[pltpu.VMEM((tm, d), F32)],
        compiler_params=_params("parallel", "parallel", "arbitrary"),
        name="moe_experts",
    )(h, wd, x, gate, w_gate_up, w_down, sh_gate_up, sh_down, final_g.reshape(1, d))


def _pad_w_in(w):
    cuts = (0, 1056, 1984, 3016, 4056)
    widths = (WA, WB, WC, WD)
    parts = []
    for lo, hi, wd in zip(cuts[:-1], cuts[1:], widths):
        parts.append(jnp.pad(w[:, lo:hi], ((0, 0), (0, wd - (hi - lo)))))
    return jnp.concatenate(parts, axis=1).astype(BF16)


def kernel(x, c, ctx, c_ctx, norm1_g, norm2_g, w_mod, b_mod, w_in, w_out, gla_gate_w2, gla_gate_b, gla_norm_g, rwkv_w_w2, rwkv_w0, rwkv_a_w2, rwkv_a0, rwkv_g_w2, rwkv_k_k, rwkv_k_a, rwkv_r_k, rwkv_ln_g, rwkv_ln_b, ssm_conv_w, ssm_conv_b, ssm_A_log, ssm_dt_bias, ssm_D, ssm_norm_g, dn_conv_w, dn_A_log, dn_dt_bias, dn_norm_g, router_w, router_bias, exp_w_gate, exp_w_up, exp_w_down, sh_w_gate, sh_w_up, sh_w_down, final_norm_g):
    bsz, n_lat, d = x.shape
    n_ctx = ctx.shape[1]
    depth = w_in.shape[0]
    rows = n_lat // GRID_W
    x = x.astype(F32)
    ctx = ctx.astype(F32)
    c_rows = jnp.concatenate([c, c_ctx[None, :], jnp.zeros((8 - bsz - 1, d), c.dtype)], axis=0).astype(F32)
    zero_init = (jnp.zeros((bsz, 2, HW, HW), F32), jnp.zeros((bsz, 2, HW, HW), F32),
                 jnp.zeros((bsz, 2, HW, SSM_STATE), F32), jnp.zeros((bsz, 2, HW, HW), F32))
    for i in range(depth):
        last = i == depth - 1
        lp = dict(gla_gate_w2=gla_gate_w2[i], gla_gate_b=gla_gate_b[i], rwkv_w_w2=rwkv_w_w2[i],
                  rwkv_w0=rwkv_w0[i], rwkv_a_w2=rwkv_a_w2[i], rwkv_a0=rwkv_a0[i],
                  rwkv_g_w2=rwkv_g_w2[i], rwkv_k_k=rwkv_k_k[i], rwkv_k_a=rwkv_k_a[i],
                  ssm_A_log=ssm_A_log[i], ssm_dt_bias=ssm_dt_bias[i],
                  dn_A_log=dn_A_log[i], dn_dt_bias=dn_dt_bias[i])
        mp = _mixer_params(lp)
        vecs = jnp.stack([gla_norm_g[i], rwkv_a0[i], rwkv_k_a[i], rwkv_r_k[i].reshape(GROUP),
                          rwkv_ln_g[i], rwkv_ln_b[i], jnp.repeat(ssm_D[i], HEAD_DIM), ssm_norm_g[i],
                          dn_norm_g[i]] + [jnp.zeros((GROUP,), F32)] * 7).astype(F32)
        w_in_pad = _pad_w_in(w_in[i])
        w_out_b = w_out[i].astype(BF16)
        w_gate_up = jnp.concatenate([exp_w_gate[i], exp_w_up[i]], axis=-1).astype(BF16)
        w_down = exp_w_down[i].astype(BF16)
        sh_gate_up = jnp.concatenate([sh_w_gate[i], sh_w_up[i]], axis=-1).astype(BF16)
        sh_down = sh_w_down[i].astype(BF16)
        dn_conv_b = jnp.zeros((3 * GROUP,), F32)

        mod = _modulation(c_rows, w_mod[i].astype(F32), b_mod[i].astype(F32))
        mod_x = [m[:, None, :] for m in jnp.split(mod[:bsz], 6, axis=-1)]
        mod_c = [jnp.broadcast_to(m[None], (bsz, 1, d)) for m in jnp.split(mod[bsz:bsz + 1], 6, axis=-1)]

        def mix(tokens, mods, init, g_rows, g_cols):
            proj = _in_projection(tokens, norm1_g[i], mods[0], mods[1], w_in_pad)
            convs = (_grid_conv(proj[2], GROUP, ssm_conv_w[i], ssm_conv_b[i], g_rows, g_cols),
                     _grid_conv(proj[3], 0, dn_conv_w[i], dn_conv_b, g_rows, g_cols))
            outs, states = _token_mixers(proj, convs, mp, init)
            return proj, convs, outs, states

        proj_c, convs_c, outs_c, ctx_states = mix(ctx, mod_c, zero_init, 1, n_ctx)
        proj_x, convs_x, outs_x, _ = mix(x, mod_x, ctx_states, rows, GRID_W)
        x = _out_projection(x, mod_x[2], outs_x, proj_x, convs_x[0][0], vecs,
                            mp['rwkv_wa'], mp['rwkv_wg'], w_out_b)

        def ffn(tokens, mods, final, flatten=False):
            shape = tokens.shape
            if flatten:
                tokens = tokens.reshape(1, shape[0] * shape[1], d)
                mods = [m[:1] for m in mods]
            h, wd = _route(tokens, norm2_g[i], mods[3], mods[4], router_w[i].astype(F32),
                           router_bias[i].astype(F32))
            out = _experts(h, wd, tokens, mods[5], w_gate_up, w_down, sh_gate_up, sh_down,
                           final_norm_g.astype(F32), final)
            return out.reshape(shape)

        x = ffn(x, mod_x, last)
        if not last:
            ctx = _out_projection(ctx, mod_c[2], outs_c, proj_c, convs_c[0][0], vecs,
                                  mp['rwkv_wa'], mp['rwkv_wg'], w_out_b)
            ctx = ffn(ctx, mod_c, False, flatten=True)
    return x
```

```python
import functools

import jax
import jax.numpy as jnp
from jax import lax
from jax.experimental import pallas as pl
from jax.experimental.pallas import tpu as pltpu

F32 = jnp.float32
BF16 = jnp.bfloat16

D_MODEL = 1024
GROUP = 256
N_HEADS = 4
HEAD_DIM = 64
CHUNK = 64
GRID_W = 64
SSM_STATE = 128
N_EXPERTS = 64
TOP_K = 8
EXPERT_DIM = 256
ROUTED_SCALE = 2.5
RMS_EPS = 1e-6
RWKV_GN_EPS = 64e-5
GLA_GATE_NORM = 16.0
LANE = 128
CONV_HALO = 128
VMEM_LIMIT = 56 * 1024 * 1024

WA = 4 * GROUP + LANE
WB = 4 * GROUP
WC = 4 * GROUP + LANE
WD = 4 * GROUP + LANE
W_IN_PAD = WA + WB + WC + WD

NT = (((1,), (1,)), ((), ()))
TN = (((0,), (0,)), ((), ()))


def _dot(a, b):
    return jnp.dot(a, b, preferred_element_type=F32)


def _dot_nt(a, b):
    return lax.dot_general(a, b, NT, preferred_element_type=F32)


def _dot_tn(a, b):
    return lax.dot_general(a, b, TN, preferred_element_type=F32)


def _split3(x):
    hi = x.astype(BF16)
    r1 = x - hi.astype(F32)
    mid = r1.astype(BF16)
    lo = (r1 - mid.astype(F32)).astype(BF16)
    return hi, mid, lo


def _dot_sel_l(sel, x):
    hi, mid, lo = _split3(x)
    return _dot(sel, hi) + _dot(sel, mid) + _dot(sel, lo)


def _dot_sel_r(x, sel):
    hi, mid, lo = _split3(x)
    return _dot(hi, sel) + _dot(mid, sel) + _dot(lo, sel)


def _dot3(a, b):
    ah, am, _ = _split3(a)
    bh, bm, _ = _split3(b)
    return _dot(ah, bh) + (_dot(ah, bm) + _dot(am, bh))


def _softplus(x):
    return jnp.maximum(x, 0.0) + jnp.log1p(jnp.exp(-jnp.abs(x)))


def _sigmoid(x):
    return 1.0 / (1.0 + jnp.exp(-x))


def _silu(x):
    return x * _sigmoid(x)


def _head_ones():
    r = lax.broadcasted_iota(jnp.int32, (GROUP, GROUP), 0) // HEAD_DIM
    c = lax.broadcasted_iota(jnp.int32, (GROUP, GROUP), 1) // HEAD_DIM
    return jnp.where(r == c, 1.0, 0.0).astype(BF16)


def _dot_stat(x, sel):
    hi = x.astype(BF16)
    lo = (x - hi.astype(F32)).astype(BF16)
    return _dot(hi, sel) + _dot(lo, sel)


def _head_sum(x):
    return _dot_sel_r(x, _head_ones())


def _head_stat(x):
    return _dot_stat(x, _head_ones())


def _params(*sem):
    return pltpu.CompilerParams(dimension_semantics=sem, vmem_limit_bytes=VMEM_LIMIT)


def _mod_kernel(c_ref, w_ref, b_ref, o_ref):
    o_ref[...] = _dot3(_silu(c_ref[...]), w_ref[...]) + b_ref[...]


def _modulation(c_rows, w, b):
    m, d = c_rows.shape
    n = w.shape[1]
    tn = 1536
    return pl.pallas_call(
        _mod_kernel,
        grid=(n // tn,),
        in_specs=[pl.BlockSpec((m, d), lambda j: (0, 0)),
                  pl.BlockSpec((d, tn), lambda j: (0, j)),
                  pl.BlockSpec((1, tn), lambda j: (0, j))],
        out_specs=pl.BlockSpec((m, tn), lambda j: (0, j)),
        out_shape=jax.ShapeDtypeStruct((m, n), F32),
        compiler_params=_params("parallel"),
        name="modulation",
    )(c_rows, w, b.reshape(1, n))


def _modulated_norm(x, g, shift, scale):
    y = x * lax.rsqrt(jnp.mean(x * x, axis=-1, keepdims=True) + RMS_EPS)
    return y * g * (1.0 + scale) + shift


def _inproj_kernel(x_ref, g_ref, sh_ref, sc_ref, w_ref, oa_ref, ob_ref, oc_ref, od_ref):
    h = _modulated_norm(x_ref[...], g_ref[...], sh_ref[...], sc_ref[...]).astype(BF16)
    y = _dot(h, w_ref[...])
    oa_ref[...] = y[:, :WA]
    ob_ref[...] = y[:, WA:WA + WB]
    oc_ref[...] = y[:, WA + WB:WA + WB + WC]
    od_ref[...] = y[:, WA + WB + WC:]


def _in_projection(x, g, shift, scale, w_pad):
    bsz, seq, d = x.shape
    tm = 256
    vec = pl.BlockSpec((None, 1, d), lambda b, i: (b, 0, 0))
    out = lambda w: pl.BlockSpec((None, tm, w), lambda b, i: (b, i, 0))
    shp = lambda w: jax.ShapeDtypeStruct((bsz, seq, w), F32)
    return pl.pallas_call(
        _inproj_kernel,
        grid=(bsz, seq // tm),
        in_specs=[pl.BlockSpec((None, tm, d), lambda b, i: (b, i, 0)),
                  pl.BlockSpec((1, d), lambda b, i: (0, 0)),
                  vec, vec,
                  pl.BlockSpec((d, W_IN_PAD), lambda b, i: (0, 0))],
        out_specs=[out(WA), out(WB), out(WC), out(WD)],
        out_shape=[shp(WA), shp(WB), shp(WC), shp(WD)],
        compiler_params=_params("parallel", "parallel"),
        name="in_projection",
    )(x, g.reshape(1, d), shift, scale, w_pad)


def _conv_kernel(prev_ref, cur_ref, next_ref, w_ref, b_ref, o_ref, ext_ref, *, cols, taps_r, n_tiles):
    i = pl.program_id(1)
    tm, ch = cur_ref.shape
    ext_ref[CONV_HALO:CONV_HALO + tm, :] = cur_ref[...]
    ext_ref[:CONV_HALO, :] = jnp.where(i > 0, prev_ref[...], 0.0)
    ext_ref[CONV_HALO + tm:, :] = jnp.where(i < n_tiles - 1, next_ref[...], 0.0)
    rt = 128
    for r0 in range(0, tm, rt):
        col = (lax.broadcasted_iota(jnp.int32, (rt, LANE), 0) + r0) % cols
        inside = {-1: col >= 1, 1: col < cols - 1}
        for c0 in range(0, ch, LANE):
            acc = jnp.zeros((rt, LANE), F32) + b_ref[:, c0:c0 + LANE]
            for dc in (-1, 0, 1):
                part = jnp.zeros((rt, LANE), F32)
                for dr in taps_r:
                    off = CONV_HALO + r0 + dr * cols + dc
                    k = (dr + 1) * 3 + (dc + 1)
                    part = part + ext_ref[off:off + rt, c0:c0 + LANE] * w_ref[k:k + 1, c0:c0 + LANE]
                acc = acc + (part if dc == 0 else jnp.where(inside[dc], part, 0.0))
            o_ref[r0:r0 + rt, c0:c0 + LANE] = _silu(acc)


def _grid_conv(proj, lane0, w, b, rows, cols):
    bsz, seq, _ = proj.shape
    tm = min(seq, 512)
    n_tiles = seq // tm
    hpt = tm // CONV_HALO
    n_halo = seq // CONV_HALO
    reach = cols + 1 if rows > 1 else 1
    assert seq == rows * cols and tm % cols == 0 and reach <= CONV_HALO and lane0 % GROUP == 0
    kern = functools.partial(_conv_kernel, cols=cols, taps_r=(-1, 0, 1) if rows > 1 else (0,),
                             n_tiles=n_tiles)
    w9 = w.reshape(9, 3 * GROUP)
    b1 = b.reshape(1, 3 * GROUP)
    outs = []
    for j in range(3):
        lb = lane0 // GROUP + j
        outs.append(pl.pallas_call(
            kern,
            grid=(bsz, n_tiles),
            in_specs=[
                pl.BlockSpec((None, CONV_HALO, GROUP),
                             lambda bi, i, lb=lb: (bi, jnp.maximum(i * hpt - 1, 0), lb)),
                pl.BlockSpec((None, tm, GROUP), lambda bi, i, lb=lb: (bi, i, lb)),
                pl.BlockSpec((None, CONV_HALO, GROUP),
                             lambda bi, i, lb=lb: (bi, jnp.minimum((i + 1) * hpt, n_halo - 1), lb)),
                pl.BlockSpec((9, GROUP), lambda bi, i, j=j: (0, j)),
                pl.BlockSpec((1, GROUP), lambda bi, i, j=j: (0, j))],
            out_specs=pl.BlockSpec((None, tm, GROUP), lambda bi, i: (bi, i, 0)),
            out_shape=jax.ShapeDtypeStruct((bsz, seq, GROUP), F32),
            scratch_shapes=[pltpu.VMEM((tm + 2 * CONV_HALO, GROUP), F32)],
            compiler_params=_params("parallel", "parallel"),
            name="grid_conv",
        )(proj, proj, proj, w9, b1))
    return outs


HW = N_HEADS * CHUNK


def _token_masks(direction, width):
    r = lax.broadcasted_iota(jnp.int32, (CHUNK, width), 0)
    c = lax.broadcasted_iota(jnp.int32, (CHUNK, width), 1) % CHUNK
    diff = (r - c) * (1 - 2 * direction)
    return diff >= 0, diff > 0


def _stacked_masks(direction):
    r = lax.broadcasted_iota(jnp.int32, (HW, HW), 0)
    c = lax.broadcasted_iota(jnp.int32, (HW, HW), 1)
    same = (r // CHUNK) == (c // CHUNK)
    diff = (r - c) * (1 - 2 * direction)
    return same, same & (diff >= 0), same & (diff > 0), r == c


def _tile4(x):
    return jnp.concatenate([x] * N_HEADS, axis=0)


def _stack4(x, same):
    return jnp.where(same, _tile4(x), 0.0)


def _unstack4(y):
    return (y[:CHUNK] + y[CHUNK:2 * CHUNK]) + (y[2 * CHUNK:3 * CHUNK] + y[3 * CHUNK:])


def _neumann_stages(ns, eye):
    ts = [eye + n for n in ns]
    ps = [n.astype(BF16) for n in ns]
    for _ in range(5):
        ps = [_dot(p, p).astype(BF16) for p in ps]
        yield
        for j, (t, p) in enumerate(zip(ts, ps)):
            ts[j] = t + _dot(t.astype(BF16), p)
        yield
    ns[:] = ts


def _cum_rows_cols(g, incl_f, incl_t_f):
    cc = _dot_sel_l(incl_f, g)
    ones = jnp.ones((CHUNK, CHUNK), BF16)
    rr = _dot_sel_l(ones, g * jnp.concatenate([incl_t_f] * N_HEADS, axis=1))
    return cc, rr


DIRS = (0, 1)
CHUNKS_PER_STEP = 2
STEP_ROWS = CHUNKS_PER_STEP * CHUNK


def _visit_order(direction):
    order = tuple(range(CHUNKS_PER_STEP))
    return order if direction == 0 else order[::-1]


def _wave(k):
    return [(d, _visit_order(d)[k]) for d in DIRS]


def _rows(ref, c):
    return ref[c * CHUNK:(c + 1) * CHUNK, :]


def _gla_stages(wave, data, w2_ref, b_ref, s_scr, o_refs):
    prep = {}
    for d, c in _wave(wave):
        qkv_ref, tail_ref = data[d]
        same, incl4, _, _ = _stacked_masks(d)
        incl_t, _ = _token_masks(d, CHUNK)
        p = _rows(qkv_ref, c)
        q = p[:, :GROUP] * HEAD_DIM ** -0.5
        k = p[:, GROUP:2 * GROUP]
        v = p[:, 2 * GROUP:]
        zg = _dot(_rows(tail_ref, c), w2_ref[d]) + b_ref[d]
        gk = (jnp.minimum(zg, 0.0) - jnp.log1p(jnp.exp(-jnp.abs(zg)))) / GLA_GATE_NORM
        bc = _dot_sel_l(jnp.where(incl_t, 1.0, 0.0).astype(BF16), gk)
        bt = jnp.sum(gk, axis=0, keepdims=True)
        prep[d, c] = dict(
            same=same, incl4=incl4,
            qt4=_stack4(q * jnp.exp(bc), same).astype(BF16),
            kt=_tile4(k * jnp.exp(-bc)).astype(BF16),
            ks=_tile4(k * jnp.exp(bt - bc)).astype(BF16),
            v4=_stack4(v, same).astype(BF16), dec=jnp.exp(bt))
        yield
    att = {key: jnp.where(p['incl4'], _dot_nt(p['qt4'], p['kt']), 0.0).astype(BF16)
           for key, p in prep.items()}
    yield
    intra = {key: _dot(att[key], p['v4']) for key, p in prep.items()}
    upd = {key: jnp.where(p['same'], _dot_tn(p['v4'], p['ks']), 0.0) for key, p in prep.items()}
    yield
    for (d, c), p in prep.items():
        st = s_scr[d]
        o_refs[d][c * CHUNK:(c + 1) * CHUNK, :] = _unstack4(
            intra[d, c] + _dot_nt(p['qt4'], st.astype(BF16)))
        s_scr[d] = st * p['dec'] + upd[d, c]


def _ssd_stages(wave, data, e_ref, dtb_ref, alog_ref, s_scr, o_refs):
    prep = {}
    for d, c in _wave(wave):
        tail_ref, xs_ref, bm_ref, cm_ref = data[d]
        same, _, _, _ = _stacked_masks(d)
        incl_t, _ = _token_masks(d, GROUP)
        incl_c, _ = _token_masks(d, CHUNK)
        incl_f = jnp.where(incl_c, 1.0, 0.0)
        dt = _softplus(_dot_sel_r(_rows(tail_ref, c), e_ref[d]) + dtb_ref[d])
        a = -jnp.exp(alog_ref[d]) * dt
        x = _rows(xs_ref, c) * dt
        cc, rr = _cum_rows_cols(a, incl_f.astype(BF16), incl_f.T)
        at = jnp.sum(a, axis=0, keepdims=True)
        bm = _rows(bm_ref, c)
        cm = _rows(cm_ref, c)
        by_head = lambda t: jnp.concatenate(
            [t[:, (h // 2) * SSM_STATE:(h // 2 + 1) * SSM_STATE] for h in range(N_HEADS)], axis=0)
        prep[d, c] = dict(
            lm4=_stack4(jnp.exp(jnp.where(incl_t, cc - rr, -jnp.inf)), same),
            bm4=by_head(bm).astype(BF16), cm4=by_head(cm).astype(BF16),
            x4=_stack4(x, same).astype(BF16),
            xd4=_stack4(x * jnp.exp(at - cc), same).astype(BF16),
            e4=_stack4(jnp.exp(cc), same),
            dec4=jnp.concatenate(
                [jnp.broadcast_to(jnp.exp(at[:, h * HEAD_DIM:h * HEAD_DIM + 1]), (CHUNK, SSM_STATE))
                 for h in range(N_HEADS)], axis=0))
        yield
    cb = {key: (_dot_nt(p['cm4'], p['bm4']) * p['lm4']).astype(BF16) for key, p in prep.items()}
    yield
    intra = {key: _dot(cb[key], p['x4']) for key, p in prep.items()}
    upd = {key: _dot_tn(p['xd4'], p['bm4']) for key, p in prep.items()}
    yield
    for (d, c), p in prep.items():
        st = s_scr[d]
        o_refs[d][c * CHUNK:(c + 1) * CHUNK, :] = _unstack4(
            intra[d, c] + _dot_nt(p['cm4'], st.astype(BF16)) * p['e4'])
        s_scr[d] = st * p['dec4'] + upd[d, c]


def _rwkv_stages(wave, data, w0_ref, ww_ref, a0_ref, wa_ref, kk_ref, ka_ref, s_scr, o_refs):
    prep = {}
    for d, c in _wave(wave):
        (p_ref,) = data[d]
        same, incl4, strict4, diag4 = _stacked_masks(d)
        incl_t, _ = _token_masks(d, CHUNK)
        p = _rows(p_ref, c)
        r = p[:, :GROUP]
        k = p[:, GROUP:2 * GROUP]
        v = p[:, 2 * GROUP:3 * GROUP]
        tail = p[:, 3 * GROUP:]
        wr = w0_ref[d] + _dot(jnp.tanh(tail), ww_ref[d])
        lw = -jnp.exp(-_softplus(-wr) - 0.5)
        a = _sigmoid(a0_ref[...] + _dot(tail, wa_ref[...]))
        kk = k * kk_ref[...]
        kk = kk * lax.rsqrt(_head_sum(kk * kk) + 1e-6)
        kmod = k * (1.0 + (a - 1.0) * ka_ref[...])
        bv = kk * a
        pc = _dot_sel_l(jnp.where(incl_t, 1.0, 0.0).astype(BF16), lw)
        pt = jnp.sum(lw, axis=0, keepdims=True)
        e_npc = jnp.exp(-pc)
        e_rest = jnp.exp(pt - pc)
        zr = jnp.concatenate([_stack4(-kk * jnp.exp(pc - lw), same), _stack4(r * jnp.exp(pc), same)],
                             axis=0).astype(BF16)
        prep[d, c] = dict(
            same=same, incl4=incl4, strict4=strict4, zr=zr,
            bt=_tile4(bv * e_npc).astype(BF16), kt=_tile4(kmod * e_npc).astype(BF16),
            v4=_stack4(v, same).astype(BF16),
            uvb=jnp.concatenate([_tile4(bv * e_rest), _tile4(kmod * e_rest)], axis=0).astype(BF16),
            dec=jnp.exp(pt), eye=jnp.where(diag4, 1.0, 0.0))
        yield
    keys = list(prep)
    ab = {key: _dot_nt(p['zr'], p['bt']) for key, p in prep.items()}
    ak = {key: _dot_nt(p['zr'], p['kt']) for key, p in prep.items()}
    yield
    ts = [jnp.where(prep[key]['strict4'], ab[key][:HW], 0.0) for key in keys]
    a_zk = {key: jnp.where(p['strict4'], ak[key][:HW], 0.0).astype(BF16) for key, p in prep.items()}
    a_rb = {key: jnp.where(p['incl4'], ab[key][HW:], 0.0).astype(BF16) for key, p in prep.items()}
    a_rk = {key: jnp.where(p['incl4'], ak[key][HW:], 0.0).astype(BF16) for key, p in prep.items()}
    zkv = {key: _dot(a_zk[key], p['v4']) for key, p in prep.items()}
    rkv = {key: _dot(a_rk[key], p['v4']) for key, p in prep.items()}
    yield from _neumann_stages(ts, prep[keys[0]]['eye'])
    tinv = {key: t.astype(BF16) for key, t in zip(keys, ts)}
    wu = {key: _dot(tinv[key], p['zr'][:HW]).astype(BF16) for key, p in prep.items()}
    uv = {key: _dot(tinv[key], zkv[key].astype(BF16)) for key in keys}
    yield
    st = {d: s_scr[d] for d, _ in keys}
    ub = {}
    for (d, c), p in prep.items():
        stb = st[d].astype(BF16)
        ub[d] = (_dot_nt(wu[d, c], stb) + uv[d, c]).astype(BF16)
        o_refs[d][c * CHUNK:(c + 1) * CHUNK, :] = _unstack4(
            _dot_nt(p['zr'][HW:], stb) + _dot(a_rb[d, c], ub[d]) + rkv[d, c])
    yield
    for (d, c), p in prep.items():
        upd = _dot_tn(jnp.concatenate([ub[d], p['v4']], axis=0), p['uvb'])
        s_scr[d] = st[d] * p['dec'] + jnp.where(p['same'], upd, 0.0)


def _gdn_stages(wave, data, eb_ref, ea_ref, dtb_ref, alog_ref, s_scr, o_refs):
    prep = {}
    for d, c in _wave(wave):
        tail_ref, q_ref, k_ref, v_ref = data[d]
        same, _, _, diag4 = _stacked_masks(d)
        incl_t, strict_t = _token_masks(d, GROUP)
        incl_c, _ = _token_masks(d, CHUNK)
        incl_f = jnp.where(incl_c, 1.0, 0.0)
        q = _rows(q_ref, c)
        q = q * lax.rsqrt(_head_sum(q * q) + 1e-6) * HEAD_DIM ** -0.5
        k = _rows(k_ref, c)
        k = k * lax.rsqrt(_head_sum(k * k) + 1e-6)
        v = _rows(v_ref, c)
        tail = _rows(tail_ref, c)
        beta = _sigmoid(_dot_sel_r(tail, eb_ref[d]))
        g = -jnp.exp(alog_ref[d]) * _softplus(_dot_sel_r(tail, ea_ref[d]) + dtb_ref[d])
        cc, rr = _cum_rows_cols(g, incl_f.astype(BF16), incl_f.T)
        seg = cc - rr
        gt = jnp.sum(g, axis=0, keepdims=True)
        e_cc = jnp.exp(cc)
        kb = k * beta
        prep[d, c] = dict(
            same=same,
            d_strict=_stack4(jnp.exp(jnp.where(strict_t, seg, -jnp.inf)), same),
            d_incl=_stack4(jnp.exp(jnp.where(incl_t, seg, -jnp.inf)), same),
            kb4=_stack4(kb, same).astype(BF16), kt=_tile4(k).astype(BF16),
            q4=_stack4(q, same).astype(BF16),
            rhs=jnp.concatenate([_stack4(kb * e_cc, same), _stack4(v * beta, same)], axis=1).astype(BF16),
            qd4=_stack4(q * e_cc, same).astype(BF16),
            kd4=_stack4(k * jnp.exp(gt - cc), same).astype(BF16),
            dec=jnp.exp(gt), eye=jnp.where(diag4, 1.0, 0.0))
        yield
    keys = list(prep)
    ts = [-(_dot_nt(prep[key]['kb4'], prep[key]['kt']) * prep[key]['d_strict']) for key in keys]
    a_qk = {key: (_dot_nt(p['q4'], p['kt']) * p['d_incl']).astype(BF16) for key, p in prep.items()}
    yield
    yield from _neumann_stages(ts, prep[keys[0]]['eye'])
    sol = {key: _dot(t.astype(BF16), prep[key]['rhs']) for key, t in zip(keys, ts)}
    yield
    st = {d: s_scr[d] for d, _ in keys}
    v_new = {}
    for d, c in keys:
        x = sol[d, c]
        v_new[d] = (x[:, HW:] - _dot(x[:, :HW].astype(BF16), st[d].astype(BF16))).astype(BF16)
    yield
    for (d, c), p in prep.items():
        o_refs[d][c * CHUNK:(c + 1) * CHUNK, :] = _unstack4(
            _dot(p['qd4'], st[d].astype(BF16)) + _dot(a_qk[d, c], v_new[d]))
        s_scr[d] = st[d] * p['dec'] + _dot_tn(p['kd4'], v_new[d])


WAVE_LAG = 3
N_SCAN_DATA = 11
N_SCAN_PARAMS = 15


def _scan_kernel(*refs):
    fwd = refs[:N_SCAN_DATA]
    bwd = refs[N_SCAN_DATA:2 * N_SCAN_DATA]
    params = refs[2 * N_SCAN_DATA:2 * N_SCAN_DATA + N_SCAN_PARAMS]
    rest = refs[2 * N_SCAN_DATA + N_SCAN_PARAMS:]
    inits, outs, finals, scr = rest[:4], rest[4:12], rest[12:16], rest[16:20]
    first = pl.program_id(1) == 0
    last = pl.program_id(1) == pl.num_programs(1) - 1

    @pl.when(first)
    def _():
        for s0_ref, s_scr in zip(inits, scr):
            s_scr[...] = s0_ref[...]

    both = lambda lo, hi: (fwd[lo:hi], bwd[lo:hi])
    (gla_w2, gla_b, rwkv_w0, rwkv_ww, rwkv_a0, rwkv_wa, rwkv_kk, rwkv_ka,
     ssd_e, ssd_dtb, ssd_alog, gdn_eb, gdn_ea, gdn_dtb, gdn_alog) = params
    def mixers(wave):
        return [
            _gdn_stages(wave, both(7, 11), gdn_eb, gdn_ea, gdn_dtb, gdn_alog, scr[3], outs[6:8]),
            _rwkv_stages(wave, both(2, 3), rwkv_w0, rwkv_ww, rwkv_a0, rwkv_wa, rwkv_kk, rwkv_ka, scr[1],
                         outs[2:4]),
            _gla_stages(wave, both(0, 2), gla_w2, gla_b, scr[0], outs[0:2]),
            _ssd_stages(wave, both(3, 7), ssd_e, ssd_dtb, ssd_alog, scr[2], outs[4:6]),
        ]

    live = [(k * WAVE_LAG, gen) for k in range(CHUNKS_PER_STEP) for gen in mixers(k)]
    rnd = 0
    while live:
        for entry in list(live):
            if rnd >= entry[0] and next(entry[1], live) is live:
                live.remove(entry)
        rnd += 1

    @pl.when(last)
    def _():
        for sfin_ref, s_scr in zip(finals, scr):
            sfin_ref[...] = s_scr[...]


def _token_mixers(proj, convs, mp, init):
    pa, pb, pc, pd = proj
    (xs, bm, cm), (dq, dk, dv) = convs
    tail_blk = 4 * GROUP // LANE
    data = [(pa, 3 * GROUP, 0), (pa, LANE, tail_blk), (pb, WB, 0),
            (pc, LANE, tail_blk), (xs, GROUP, 0), (bm, GROUP, 0), (cm, GROUP, 0),
            (pd, LANE, tail_blk), (dq, GROUP, 0), (dk, GROUP, 0), (dv, GROUP, 0)]
    params = [mp[k] for k in ('gla_w2', 'gla_b', 'rwkv_w0', 'rwkv_ww', 'rwkv_a0', 'rwkv_wa', 'rwkv_kk',
                              'rwkv_ka', 'ssd_e', 'ssd_dtb', 'ssd_alog', 'gdn_eb', 'gdn_ea', 'gdn_dtb',
                              'gdn_alog')]
    assert len(data) == N_SCAN_DATA and len(params) == N_SCAN_PARAMS
    bsz, seq = pa.shape[:2]
    n_steps = seq // STEP_ROWS
    in_specs, args = [], []
    for mirror in (False, True):
        for arr, width, blk in data:
            idx = (lambda b, n, blk=blk: (b, n_steps - 1 - n, blk)) if mirror else \
                  (lambda b, n, blk=blk: (b, n, blk))
            in_specs.append(pl.BlockSpec((None, STEP_ROWS, width), idx))
            args.append(arr)
    for arr in params:
        in_specs.append(pl.BlockSpec(arr.shape, lambda b, n, nd=arr.ndim: (0,) * nd))
        args.append(arr)
    st_specs = [pl.BlockSpec((None,) + s.shape[1:], lambda b, n, nd=s.ndim - 1: (b,) + (0,) * nd)
                for s in init]
    in_specs += st_specs
    args += list(init)
    o_shape = jax.ShapeDtypeStruct((bsz, seq, GROUP), F32)
    o_specs = [pl.BlockSpec((None, STEP_ROWS, GROUP), lambda b, n: (b, n, 0)),
               pl.BlockSpec((None, STEP_ROWS, GROUP), lambda b, n: (b, n_steps - 1 - n, 0))] * 4
    res = pl.pallas_call(
        _scan_kernel,
        grid=(bsz, n_steps),
        in_specs=in_specs,
        out_specs=o_specs + st_specs,
        out_shape=[o_shape] * 8 + [jax.ShapeDtypeStruct(s.shape, F32) for s in init],
        scratch_shapes=[pltpu.VMEM(s.shape[1:], F32) for s in init],
        compiler_params=_params("parallel", "arbitrary"),
        name="mixer_scans",
    )(*args)
    return tuple(res[:8]), tuple(res[8:])


def _expand_matrix(row0):
    d = jnp.arange(2)[:, None, None]
    r = jnp.arange(LANE)[None, :, None]
    c = jnp.arange(GROUP)[None, None, :]
    return (r == row0 + N_HEADS * d + c // HEAD_DIM).astype(BF16)


def _per_head_lanes(t):
    return jnp.repeat(t.astype(F32), HEAD_DIM, axis=-1)[:, None, :]


def _rows_at(w, row0, total):
    pad = [(0, 0)] * (w.ndim - 2) + [(row0, total - row0 - w.shape[-2]), (0, 0)]
    return jnp.pad(w.astype(F32), pad)


def _mixer_params(lp):
    f = lambda t: t.astype(F32).reshape(1, GROUP)
    return dict(
        gla_w2=jnp.stack([_rows_at(lp['gla_gate_w2'][0], 0, LANE), _rows_at(lp['gla_gate_w2'][1], 16, LANE)]),
        gla_b=lp['gla_gate_b'].astype(F32)[:, None, :],
        rwkv_w0=lp['rwkv_w0'].astype(F32)[:, None, :],
        rwkv_ww=jnp.stack([_rows_at(lp['rwkv_w_w2'][0], 0, GROUP), _rows_at(lp['rwkv_w_w2'][1], 32, GROUP)]),
        rwkv_a0=f(lp['rwkv_a0']),
        rwkv_wa=_rows_at(lp['rwkv_a_w2'], 64, GROUP),
        rwkv_wg=_rows_at(lp['rwkv_g_w2'], 96, GROUP),
        rwkv_kk=f(lp['rwkv_k_k']),
        rwkv_ka=f(lp['rwkv_k_a']),
        ssd_e=_expand_matrix(0),
        ssd_dtb=_per_head_lanes(lp['ssm_dt_bias']),
        ssd_alog=_per_head_lanes(lp['ssm_A_log']),
        gdn_eb=_expand_matrix(0),
        gdn_ea=_expand_matrix(8),
        gdn_dtb=_per_head_lanes(lp['dn_dt_bias']),
        gdn_alog=_per_head_lanes(lp['dn_A_log']),
    )


V_GLA_G, V_A0, V_KA, V_RK, V_LN_G, V_LN_B, V_SSM_D, V_SSM_G, V_DN_G = range(9)


def _group_ones():
    r = lax.broadcasted_iota(jnp.int32, (GROUP, GROUP), 0) // SSM_STATE
    c = lax.broadcasted_iota(jnp.int32, (GROUP, GROUP), 1) // SSM_STATE
    return jnp.where(r == c, 1.0, 0.0).astype(BF16)


def _outproj_kernel(x_ref, ga_ref, oaf_ref, oab_ref, obf_ref, obb_ref, ocf_ref, ocb_ref, odf_ref, odb_ref,
                    ag_ref, pb_ref, cz_ref, xs_ref, dz_ref, vec_ref, wa_ref, wg_ref, wout_ref, o_ref):
    vec = lambda i: vec_ref[i:i + 1, :]
    inv_hd = 1.0 / HEAD_DIM

    o = oaf_ref[...] + oab_ref[...]
    ya = o * lax.rsqrt(_head_stat(o * o) * inv_hd + RMS_EPS) * vec(V_GLA_G) * _silu(ag_ref[...])

    pb = pb_ref[...]
    r = pb[:, :GROUP]
    k = pb[:, GROUP:2 * GROUP]
    v = pb[:, 2 * GROUP:3 * GROUP]
    tail = pb[:, 3 * GROUP:]
    a = _sigmoid(vec(V_A0) + _dot(tail, wa_ref[...]))
    kmod = k * (1.0 + (a - 1.0) * vec(V_KA))
    gate = _dot(_sigmoid(tail), wg_ref[...])
    o = obf_ref[...] + obb_ref[...]
    dev = o - _head_stat(o) * inv_hd
    on = dev * lax.rsqrt(_head_stat(dev * dev) * inv_hd + RWKV_GN_EPS) * vec(V_LN_G) + vec(V_LN_B)
    yb = (on + _head_stat(r * kmod * vec(V_RK)) * v) * gate

    y = (ocf_ref[...] + ocb_ref[...] + vec(V_SSM_D) * xs_ref[...]) * _silu(cz_ref[...])
    yc = y * lax.rsqrt(_dot_stat(y * y, _group_ones()) * (1.0 / SSM_STATE) + RMS_EPS) * vec(V_SSM_G)

    o = odf_ref[...] + odb_ref[...]
    yd = o * lax.rsqrt(_head_stat(o * o) * inv_hd + RMS_EPS) * vec(V_DN_G) * _silu(dz_ref[...])

    cat = jnp.concatenate([ya, yb, yc, yd], axis=1).astype(BF16)
    o_ref[...] = x_ref[...] + ga_ref[...] * _dot(cat, wout_ref[...])


def _out_projection(x, gate, outs, proj, xs, vecs, wa, wg, w_out):
    bsz, seq, d = x.shape
    tm = 256
    pa, pb, pc, pd = proj
    col = lambda blk: pl.BlockSpec((None, tm, GROUP), lambda b, i: (b, i, blk))
    full = lambda arr: pl.BlockSpec(arr.shape, lambda b, i: (0, 0))
    return pl.pallas_call(
        _outproj_kernel,
        grid=(bsz, seq // tm),
        in_specs=[pl.BlockSpec((None, tm, d), lambda b, i: (b, i, 0)),
                  pl.BlockSpec((None, 1, d), lambda b, i: (b, 0, 0)),
                  col(0), col(0), col(0), col(0), col(0), col(0), col(0), col(0),
                  col(3),
                  pl.BlockSpec((None, tm, WB), lambda b, i: (b, i, 0)),
                  col(0), col(0), col(3),
                  full(vecs), full(wa), full(wg), full(w_out)],
        out_specs=pl.BlockSpec((None, tm, d), lambda b, i: (b, i, 0)),
        out_shape=jax.ShapeDtypeStruct(x.shape, F32),
        compiler_params=_params("parallel", "parallel"),
        name="out_projection",
    )(x, gate, *outs, pa, pb, pc, xs, pd, vecs, wa, wg, w_out)


def _route_kernel(x_ref, g_ref, sh_ref, sc_ref, rw_ref, rb_ref, h_ref, wd_ref):
    h = _modulated_norm(x_ref[...], g_ref[...], sh_ref[...], sc_ref[...]).astype(BF16)
    h_ref[...] = h
    scores = _sigmoid(_dot(h, rw_ref[...]))
    sel = scores + rb_ref[...]
    lane = lax.broadcasted_iota(jnp.int32, sel.shape, 1)
    chosen = jnp.zeros(sel.shape, F32)
    for _ in range(TOP_K):
        m = jnp.max(sel, axis=-1, keepdims=True)
        first = jnp.min(jnp.where(sel == m, lane, N_EXPERTS), axis=-1, keepdims=True)
        pick = lane == first
        chosen = jnp.where(pick, 1.0, chosen)
        sel = jnp.where(pick, -jnp.inf, sel)
    w = scores * chosen
    wd_ref[...] = w / jnp.sum(w, axis=-1, keepdims=True) * ROUTED_SCALE


def _route(x, g, shift, scale, router_w, router_bias):
    bsz, seq, d = x.shape
    tm = 256
    vec = pl.BlockSpec((None, 1, d), lambda b, i: (b, 0, 0))
    return pl.pallas_call(
        _route_kernel,
        grid=(bsz, seq // tm),
        in_specs=[pl.BlockSpec((None, tm, d), lambda b, i: (b, i, 0)),
                  pl.BlockSpec((1, d), lambda b, i: (0, 0)),
                  vec, vec,
                  pl.BlockSpec((d, N_EXPERTS), lambda b, i: (0, 0)),
                  pl.BlockSpec((1, N_EXPERTS), lambda b, i: (0, 0))],
        out_specs=[pl.BlockSpec((None, tm, d), lambda b, i: (b, i, 0)),
                   pl.BlockSpec((None, tm, N_EXPERTS), lambda b, i: (b, i, 0))],
        out_shape=[jax.ShapeDtypeStruct((bsz, seq, d), BF16),
                   jax.ShapeDtypeStruct((bsz, seq, N_EXPERTS), F32)],
        compiler_params=_params("parallel", "parallel"),
        name="moe_route",
    )(x, g.reshape(1, d), shift, scale, router_w, router_bias.reshape(1, N_EXPERTS))


def _swiglu_hidden(h, w_gate, w_up):
    return _silu(_dot(h, w_gate)) * _dot(h, w_up)


def _experts_kernel(h_ref, wd_ref, x_ref, gf_ref, wg_ref, wu_ref, wdn_ref, sg_ref, su_ref, sdn_ref, fg_ref,
                    o_ref, acc_ref, *, final_norm):
    grp = pl.program_id(2)
    per_step = wg_ref.shape[0]
    h = h_ref[...]

    @pl.when(grp == 0)
    def _():
        acc_ref[...] = _dot(_swiglu_hidden(h, sg_ref[...], su_ref[...]).astype(BF16), sdn_ref[...])

    wd = wd_ref[...]
    lane = lax.broadcasted_iota(jnp.int32, wd.shape, 1)
    acts = []
    for j in range(per_step):
        w_e = jnp.sum(jnp.where(lane == grp * per_step + j, wd, 0.0), axis=-1, keepdims=True)
        acts.append((_swiglu_hidden(h, wg_ref[j], wu_ref[j]) * w_e).astype(BF16))
    w_down = wdn_ref[...].reshape(per_step * EXPERT_DIM, wdn_ref.shape[-1])
    acc_ref[...] += _dot(jnp.concatenate(acts, axis=1), w_down)

    @pl.when(grp == pl.num_programs(2) - 1)
    def _():
        y = x_ref[...] + gf_ref[...] * acc_ref[...]
        if final_norm:
            y = y * lax.rsqrt(jnp.mean(y * y, axis=-1, keepdims=True) + RMS_EPS) * fg_ref[...]
        o_ref[...] = y


EXPERTS_PER_STEP = 4


def _experts(h, wd, x, gate, layer, weights, final_g, final_norm):
    bsz, seq, d = x.shape
    tm = min(seq, 1024)
    eps = EXPERTS_PER_STEP
    w_gate, w_up, w_down, sh_gate, sh_up, sh_down = weights
    tok = lambda w: pl.BlockSpec((None, tm, w), lambda b, i, e: (b, i, 0))
    routed = lambda r, c: pl.BlockSpec((None, eps, r, c), lambda b, i, e: (layer, e, 0, 0))
    shared = lambda r, c: pl.BlockSpec((None, r, c), lambda b, i, e: (layer, 0, 0))
    return pl.pallas_call(
        functools.partial(_experts_kernel, final_norm=final_norm),
        grid=(bsz, seq // tm, N_EXPERTS // eps),
        in_specs=[tok(d), tok(N_EXPERTS), tok(d),
                  pl.BlockSpec((None, 1, d), lambda b, i, e: (b, 0, 0)),
                  routed(d, EXPERT_DIM), routed(d, EXPERT_DIM), routed(EXPERT_DIM, d),
                  shared(d, EXPERT_DIM), shared(d, EXPERT_DIM), shared(EXPERT_DIM, d),
                  pl.BlockSpec((1, d), lambda b, i, e: (0, 0))],
        out_specs=tok(d),
        out_shape=jax.ShapeDtypeStruct(x.shape, F32),
        scratch_shapes=[pltpu.VMEM((tm, d), F32)],
        compiler_params=_params("parallel", "parallel", "arbitrary"),
        name="moe_experts",
    )(h, wd, x, gate, w_gate, w_up, w_down, sh_gate, sh_up, sh_down, final_g.reshape(1, d))


def _pad_w_in(w):
    cuts = (0, 1056, 1984, 3016, 4056)
    widths = (WA, WB, WC, WD)
    parts = []
    for lo, hi, wd in zip(cuts[:-1], cuts[1:], widths):
        parts.append(jnp.pad(w[:, lo:hi], ((0, 0), (0, wd - (hi - lo)))))
    return jnp.concatenate(parts, axis=1).astype(BF16)


def kernel(x, c, ctx, c_ctx, norm1_g, norm2_g, w_mod, b_mod, w_in, w_out, gla_gate_w2, gla_gate_b, gla_norm_g, rwkv_w_w2, rwkv_w0, rwkv_a_w2, rwkv_a0, rwkv_g_w2, rwkv_k_k, rwkv_k_a, rwkv_r_k, rwkv_ln_g, rwkv_ln_b, ssm_conv_w, ssm_conv_b, ssm_A_log, ssm_dt_bias, ssm_D, ssm_norm_g, dn_conv_w, dn_A_log, dn_dt_bias, dn_norm_g, router_w, router_bias, exp_w_gate, exp_w_up, exp_w_down, sh_w_gate, sh_w_up, sh_w_down, final_norm_g):
    bsz, n_lat, d = x.shape
    n_ctx = ctx.shape[1]
    depth = w_in.shape[0]
    rows = n_lat // GRID_W
    x = x.astype(F32)
    ctx = ctx.astype(F32)
    c_rows = jnp.concatenate([c, c_ctx[None, :], jnp.zeros((8 - bsz - 1, d), c.dtype)], axis=0).astype(F32)
    zero_init = (jnp.zeros((bsz, 2, HW, HW), F32), jnp.zeros((bsz, 2, HW, HW), F32),
                 jnp.zeros((bsz, 2, HW, SSM_STATE), F32), jnp.zeros((bsz, 2, HW, HW), F32))
    moe_weights = tuple(w.astype(BF16) for w in (exp_w_gate, exp_w_up, exp_w_down,
                                                  sh_w_gate, sh_w_up, sh_w_down))
    for i in range(depth):
        last = i == depth - 1
        lp = dict(gla_gate_w2=gla_gate_w2[i], gla_gate_b=gla_gate_b[i], rwkv_w_w2=rwkv_w_w2[i],
                  rwkv_w0=rwkv_w0[i], rwkv_a_w2=rwkv_a_w2[i], rwkv_a0=rwkv_a0[i],
                  rwkv_g_w2=rwkv_g_w2[i], rwkv_k_k=rwkv_k_k[i], rwkv_k_a=rwkv_k_a[i],
                  ssm_A_log=ssm_A_log[i], ssm_dt_bias=ssm_dt_bias[i],
                  dn_A_log=dn_A_log[i], dn_dt_bias=dn_dt_bias[i])
        mp = _mixer_params(lp)
        vecs = jnp.stack([gla_norm_g[i], rwkv_a0[i], rwkv_k_a[i], rwkv_r_k[i].reshape(GROUP),
                          rwkv_ln_g[i], rwkv_ln_b[i], jnp.repeat(ssm_D[i], HEAD_DIM), ssm_norm_g[i],
                          dn_norm_g[i]] + [jnp.zeros((GROUP,), F32)] * 7).astype(F32)
        w_in_pad = _pad_w_in(w_in[i])
        w_out_b = w_out[i].astype(BF16)
        dn_conv_b = jnp.zeros((3 * GROUP,), F32)

        mod = _modulation(c_rows, w_mod[i].astype(F32), b_mod[i].astype(F32))
        mod_x = [m[:, None, :] for m in jnp.split(mod[:bsz], 6, axis=-1)]
        mod_c = [jnp.broadcast_to(m[None], (bsz, 1, d)) for m in jnp.split(mod[bsz:bsz + 1], 6, axis=-1)]

        def mix(tokens, mods, init, g_rows, g_cols):
            proj = _in_projection(tokens, norm1_g[i], mods[0], mods[1], w_in_pad)
            convs = (_grid_conv(proj[2], GROUP, ssm_conv_w[i], ssm_conv_b[i], g_rows, g_cols),
                     _grid_conv(proj[3], 0, dn_conv_w[i], dn_conv_b, g_rows, g_cols))
            outs, states = _token_mixers(proj, convs, mp, init)
            return proj, convs, outs, states

        proj_c, convs_c, outs_c, ctx_states = mix(ctx, mod_c, zero_init, 1, n_ctx)
        proj_x, convs_x, outs_x, _ = mix(x, mod_x, ctx_states, rows, GRID_W)
        x = _out_projection(x, mod_x[2], outs_x, proj_x, convs_x[0][0], vecs,
                            mp['rwkv_wa'], mp['rwkv_wg'], w_out_b)

        def ffn(tokens, mods, final, flatten=False):
            shape = tokens.shape
            if flatten:
                tokens = tokens.reshape(1, shape[0] * shape[1], d)
                mods = [m[:1] for m in mods]
            h, wd = _route(tokens, norm2_g[i], mods[3], mods[4], router_w[i].astype(BF16),
                           router_bias[i].astype(F32))
            out = _experts(h, wd, tokens, mods[5], i, moe_weights, final_norm_g.astype(F32), final)
            return out.reshape(shape)

        x = ffn(x, mod_x, last)
        if not last:
            ctx = _out_projection(ctx, mod_c[2], outs_c, proj_c, convs_c[0][0], vecs,
                                  mp['rwkv_wa'], mp['rwkv_wg'], w_out_b)
            ctx = ffn(ctx, mod_c, False, flatten=True)
    return x
```

```python
import functools

import jax
import jax.numpy as jnp
from jax import lax
from jax.experimental import pallas as pl
from jax.experimental.pallas import tpu as pltpu

F32 = jnp.float32
BF16 = jnp.bfloat16

D_MODEL = 1024
GROUP = 256
N_HEADS = 4
HEAD_DIM = 64
CHUNK = 64
GRID_W = 64
SSM_STATE = 128
N_EXPERTS = 64
TOP_K = 8
EXPERT_DIM = 256
ROUTED_SCALE = 2.5
RMS_EPS = 1e-6
RWKV_GN_EPS = 64e-5
GLA_GATE_NORM = 16.0
LANE = 128
CONV_HALO = 128
VMEM_LIMIT = 56 * 1024 * 1024

WA = 4 * GROUP + LANE
WB = 4 * GROUP
WC = 4 * GROUP + LANE
WD = 4 * GROUP + LANE
W_IN_PAD = WA + WB + WC + WD

NT = (((1,), (1,)), ((), ()))
TN = (((0,), (0,)), ((), ()))


def _dot(a, b):
    return jnp.dot(a, b, preferred_element_type=F32)


def _dot_nt(a, b):
    return lax.dot_general(a, b, NT, preferred_element_type=F32)


def _dot_tn(a, b):
    return lax.dot_general(a, b, TN, preferred_element_type=F32)


def _split3(x):
    hi = x.astype(BF16)
    r1 = x - hi.astype(F32)
    mid = r1.astype(BF16)
    lo = (r1 - mid.astype(F32)).astype(BF16)
    return hi, mid, lo


def _dot_sel_l(sel, x):
    hi, mid, lo = _split3(x)
    return _dot(sel, hi) + _dot(sel, mid) + _dot(sel, lo)


def _dot_sel_r(x, sel):
    hi, mid, lo = _split3(x)
    return _dot(hi, sel) + _dot(mid, sel) + _dot(lo, sel)


def _dot3(a, b):
    ah, am, _ = _split3(a)
    bh, bm, _ = _split3(b)
    return _dot(ah, bh) + (_dot(ah, bm) + _dot(am, bh))


def _softplus(x):
    return jnp.maximum(x, 0.0) + jnp.log1p(jnp.exp(-jnp.abs(x)))


def _sigmoid(x):
    return 1.0 / (1.0 + jnp.exp(-x))


def _silu(x):
    return x * _sigmoid(x)


def _head_ones():
    r = lax.broadcasted_iota(jnp.int32, (GROUP, GROUP), 0) // HEAD_DIM
    c = lax.broadcasted_iota(jnp.int32, (GROUP, GROUP), 1) // HEAD_DIM
    return jnp.where(r == c, 1.0, 0.0).astype(BF16)


def _dot_stat(x, sel):
    hi = x.astype(BF16)
    lo = (x - hi.astype(F32)).astype(BF16)
    return _dot(hi, sel) + _dot(lo, sel)


def _head_sum(x):
    return _dot_sel_r(x, _head_ones())


def _head_stat(x):
    return _dot_stat(x, _head_ones())


def _params(*sem):
    return pltpu.CompilerParams(dimension_semantics=sem, vmem_limit_bytes=VMEM_LIMIT)


def _mod_kernel(c_ref, w_ref, b_ref, o_ref):
    o_ref[...] = _dot3(_silu(c_ref[...]), w_ref[...]) + b_ref[...]


def _modulation(c_rows, w, b):
    m, d = c_rows.shape
    n = w.shape[1]
    tn = 1536
    return pl.pallas_call(
        _mod_kernel,
        grid=(n // tn,),
        in_specs=[pl.BlockSpec((m, d), lambda j: (0, 0)),
                  pl.BlockSpec((d, tn), lambda j: (0, j)),
                  pl.BlockSpec((1, tn), lambda j: (0, j))],
        out_specs=pl.BlockSpec((m, tn), lambda j: (0, j)),
        out_shape=jax.ShapeDtypeStruct((m, n), F32),
        compiler_params=_params("parallel"),
        name="modulation",
    )(c_rows, w, b.reshape(1, n))


def _modulated_norm(x, g, shift, scale):
    y = x * lax.rsqrt(jnp.mean(x * x, axis=-1, keepdims=True) + RMS_EPS)
    return y * g * (1.0 + scale) + shift


def _inproj_kernel(x_ref, g_ref, sh_ref, sc_ref, w_ref, oa_ref, ob_ref, oc_ref, od_ref):
    h = _modulated_norm(x_ref[...], g_ref[...], sh_ref[...], sc_ref[...]).astype(BF16)
    y = _dot(h, w_ref[...])
    oa_ref[...] = y[:, :WA]
    ob_ref[...] = y[:, WA:WA + WB]
    oc_ref[...] = y[:, WA + WB:WA + WB + WC]
    od_ref[...] = y[:, WA + WB + WC:]


def _in_projection(x, g, shift, scale, w_pad):
    bsz, seq, d = x.shape
    tm = 256
    vec = pl.BlockSpec((None, 1, d), lambda b, i: (b, 0, 0))
    out = lambda w: pl.BlockSpec((None, tm, w), lambda b, i: (b, i, 0))
    shp = lambda w: jax.ShapeDtypeStruct((bsz, seq, w), F32)
    return pl.pallas_call(
        _inproj_kernel,
        grid=(bsz, seq // tm),
        in_specs=[pl.BlockSpec((None, tm, d), lambda b, i: (b, i, 0)),
                  pl.BlockSpec((1, d), lambda b, i: (0, 0)),
                  vec, vec,
                  pl.BlockSpec((d, W_IN_PAD), lambda b, i: (0, 0))],
        out_specs=[out(WA), out(WB), out(WC), out(WD)],
        out_shape=[shp(WA), shp(WB), shp(WC), shp(WD)],
        compiler_params=_params("parallel", "parallel"),
        name="in_projection",
    )(x, g.reshape(1, d), shift, scale, w_pad)


def _conv_kernel(prev_ref, cur_ref, next_ref, w_ref, b_ref, o_ref, ext_ref, *, cols, taps_r, n_tiles):
    i = pl.program_id(1)
    tm, ch = cur_ref.shape
    ext_ref[CONV_HALO:CONV_HALO + tm, :] = cur_ref[...]
    ext_ref[:CONV_HALO, :] = jnp.where(i > 0, prev_ref[...], 0.0)
    ext_ref[CONV_HALO + tm:, :] = jnp.where(i < n_tiles - 1, next_ref[...], 0.0)
    rt = 128
    for r0 in range(0, tm, rt):
        col = (lax.broadcasted_iota(jnp.int32, (rt, LANE), 0) + r0) % cols
        inside = {-1: col >= 1, 1: col < cols - 1}
        for c0 in range(0, ch, LANE):
            acc = jnp.zeros((rt, LANE), F32) + b_ref[:, c0:c0 + LANE]
            for dc in (-1, 0, 1):
                part = jnp.zeros((rt, LANE), F32)
                for dr in taps_r:
                    off = CONV_HALO + r0 + dr * cols + dc
                    k = (dr + 1) * 3 + (dc + 1)
                    part = part + ext_ref[off:off + rt, c0:c0 + LANE] * w_ref[k:k + 1, c0:c0 + LANE]
                acc = acc + (part if dc == 0 else jnp.where(inside[dc], part, 0.0))
            o_ref[r0:r0 + rt, c0:c0 + LANE] = _silu(acc)


def _grid_conv(proj, lane0, w, b, rows, cols):
    bsz, seq, _ = proj.shape
    tm = min(seq, 512)
    n_tiles = seq // tm
    hpt = tm // CONV_HALO
    n_halo = seq // CONV_HALO
    reach = cols + 1 if rows > 1 else 1
    assert seq == rows * cols and tm % cols == 0 and reach <= CONV_HALO and lane0 % GROUP == 0
    kern = functools.partial(_conv_kernel, cols=cols, taps_r=(-1, 0, 1) if rows > 1 else (0,),
                             n_tiles=n_tiles)
    w9 = w.reshape(9, 3 * GROUP)
    b1 = b.reshape(1, 3 * GROUP)
    outs = []
    for j in range(3):
        lb = lane0 // GROUP + j
        outs.append(pl.pallas_call(
            kern,
            grid=(bsz, n_tiles),
            in_specs=[
                pl.BlockSpec((None, CONV_HALO, GROUP),
                             lambda bi, i, lb=lb: (bi, jnp.maximum(i * hpt - 1, 0), lb)),
                pl.BlockSpec((None, tm, GROUP), lambda bi, i, lb=lb: (bi, i, lb)),
                pl.BlockSpec((None, CONV_HALO, GROUP),
                             lambda bi, i, lb=lb: (bi, jnp.minimum((i + 1) * hpt, n_halo - 1), lb)),
                pl.BlockSpec((9, GROUP), lambda bi, i, j=j: (0, j)),
                pl.BlockSpec((1, GROUP), lambda bi, i, j=j: (0, j))],
            out_specs=pl.BlockSpec((None, tm, GROUP), lambda bi, i: (bi, i, 0)),
            out_shape=jax.ShapeDtypeStruct((bsz, seq, GROUP), F32),
            scratch_shapes=[pltpu.VMEM((tm + 2 * CONV_HALO, GROUP), F32)],
            compiler_params=_params("parallel", "parallel"),
            name="grid_conv",
        )(proj, proj, proj, w9, b1))
    return outs


HW = N_HEADS * CHUNK


def _token_masks(direction, width):
    r = lax.broadcasted_iota(jnp.int32, (CHUNK, width), 0)
    c = lax.broadcasted_iota(jnp.int32, (CHUNK, width), 1) % CHUNK
    diff = (r - c) * (1 - 2 * direction)
    return diff >= 0, diff > 0


def _stacked_masks(direction):
    r = lax.broadcasted_iota(jnp.int32, (HW, HW), 0)
    c = lax.broadcasted_iota(jnp.int32, (HW, HW), 1)
    same = (r // CHUNK) == (c // CHUNK)
    diff = (r - c) * (1 - 2 * direction)
    return same, same & (diff >= 0), same & (diff > 0), r == c


def _tile4(x):
    return jnp.concatenate([x] * N_HEADS, axis=0)


def _stack4(x, same):
    return jnp.where(same, _tile4(x), 0.0)


def _unstack4(y):
    return (y[:CHUNK] + y[CHUNK:2 * CHUNK]) + (y[2 * CHUNK:3 * CHUNK] + y[3 * CHUNK:])


def _neumann_stages(ns, eye):
    ts = [eye + n for n in ns]
    ps = [n.astype(BF16) for n in ns]
    for _ in range(5):
        ps = [_dot(p, p).astype(BF16) for p in ps]
        yield
        for j, (t, p) in enumerate(zip(ts, ps)):
            ts[j] = t + _dot(t.astype(BF16), p)
        yield
    ns[:] = ts


def _cum_rows_cols(g, incl_f, incl_t_f):
    cc = _dot_sel_l(incl_f, g)
    ones = jnp.ones((CHUNK, CHUNK), BF16)
    rr = _dot_sel_l(ones, g * jnp.concatenate([incl_t_f] * N_HEADS, axis=1))
    return cc, rr


DIRS = (0, 1)
CHUNKS_PER_STEP = 4
STEP_ROWS = CHUNKS_PER_STEP * CHUNK


def _visit_order(direction):
    order = tuple(range(CHUNKS_PER_STEP))
    return order if direction == 0 else order[::-1]


def _wave(k):
    return [(d, _visit_order(d)[k]) for d in DIRS]


def _rows(ref, c):
    return ref[c * CHUNK:(c + 1) * CHUNK, :]


def _gla_stages(wave, data, w2_ref, b_ref, s_scr, o_refs):
    prep = {}
    for d, c in _wave(wave):
        qkv_ref, tail_ref = data[d]
        same, incl4, _, _ = _stacked_masks(d)
        incl_t, _ = _token_masks(d, CHUNK)
        p = _rows(qkv_ref, c)
        q = p[:, :GROUP] * HEAD_DIM ** -0.5
        k = p[:, GROUP:2 * GROUP]
        v = p[:, 2 * GROUP:]
        zg = _dot(_rows(tail_ref, c), w2_ref[d]) + b_ref[d]
        gk = (jnp.minimum(zg, 0.0) - jnp.log1p(jnp.exp(-jnp.abs(zg)))) / GLA_GATE_NORM
        bc = _dot_sel_l(jnp.where(incl_t, 1.0, 0.0).astype(BF16), gk)
        bt = jnp.sum(gk, axis=0, keepdims=True)
        prep[d, c] = dict(
            same=same, incl4=incl4,
            qt4=_stack4(q * jnp.exp(bc), same).astype(BF16),
            kt=_tile4(k * jnp.exp(-bc)).astype(BF16),
            ks=_tile4(k * jnp.exp(bt - bc)).astype(BF16),
            v4=_stack4(v, same).astype(BF16), dec=jnp.exp(bt))
        yield
    att = {key: jnp.where(p['incl4'], _dot_nt(p['qt4'], p['kt']), 0.0).astype(BF16)
           for key, p in prep.items()}
    yield
    intra = {key: _dot(att[key], p['v4']) for key, p in prep.items()}
    upd = {key: jnp.where(p['same'], _dot_tn(p['v4'], p['ks']), 0.0) for key, p in prep.items()}
    yield
    for (d, c), p in prep.items():
        st = s_scr[d]
        o_refs[d][c * CHUNK:(c + 1) * CHUNK, :] = _unstack4(
            intra[d, c] + _dot_nt(p['qt4'], st.astype(BF16)))
        s_scr[d] = st * p['dec'] + upd[d, c]


def _ssd_stages(wave, data, e_ref, dtb_ref, alog_ref, s_scr, o_refs):
    prep = {}
    for d, c in _wave(wave):
        tail_ref, xs_ref, bm_ref, cm_ref = data[d]
        same, _, _, _ = _stacked_masks(d)
        incl_t, _ = _token_masks(d, GROUP)
        incl_c, _ = _token_masks(d, CHUNK)
        incl_f = jnp.where(incl_c, 1.0, 0.0)
        dt = _softplus(_dot_sel_r(_rows(tail_ref, c), e_ref[d]) + dtb_ref[d])
        a = -jnp.exp(alog_ref[d]) * dt
        x = _rows(xs_ref, c) * dt
        cc, rr = _cum_rows_cols(a, incl_f.astype(BF16), incl_f.T)
        at = jnp.sum(a, axis=0, keepdims=True)
        bm = _rows(bm_ref, c)
        cm = _rows(cm_ref, c)
        by_head = lambda t: jnp.concatenate(
            [t[:, (h // 2) * SSM_STATE:(h // 2 + 1) * SSM_STATE] for h in range(N_HEADS)], axis=0)
        prep[d, c] = dict(
            lm4=_stack4(jnp.exp(jnp.where(incl_t, cc - rr, -jnp.inf)), same),
            bm4=by_head(bm).astype(BF16), cm4=by_head(cm).astype(BF16),
            x4=_stack4(x, same).astype(BF16),
            xd4=_stack4(x * jnp.exp(at - cc), same).astype(BF16),
            e4=_stack4(jnp.exp(cc), same),
            dec4=jnp.concatenate(
                [jnp.broadcast_to(jnp.exp(at[:, h * HEAD_DIM:h * HEAD_DIM + 1]), (CHUNK, SSM_STATE))
                 for h in range(N_HEADS)], axis=0))
        yield
    cb = {key: (_dot_nt(p['cm4'], p['bm4']) * p['lm4']).astype(BF16) for key, p in prep.items()}
    yield
    intra = {key: _dot(cb[key], p['x4']) for key, p in prep.items()}
    upd = {key: _dot_tn(p['xd4'], p['bm4']) for key, p in prep.items()}
    yield
    for (d, c), p in prep.items():
        st = s_scr[d]
        o_refs[d][c * CHUNK:(c + 1) * CHUNK, :] = _unstack4(
            intra[d, c] + _dot_nt(p['cm4'], st.astype(BF16)) * p['e4'])
        s_scr[d] = st * p['dec4'] + upd[d, c]


def _rwkv_stages(wave, data, w0_ref, ww_ref, a0_ref, wa_ref, kk_ref, ka_ref, s_scr, o_refs):
    prep = {}
    for d, c in _wave(wave):
        (p_ref,) = data[d]
        same, incl4, strict4, diag4 = _stacked_masks(d)
        incl_t, _ = _token_masks(d, CHUNK)
        p = _rows(p_ref, c)
        r = p[:, :GROUP]
        k = p[:, GROUP:2 * GROUP]
        v = p[:, 2 * GROUP:3 * GROUP]
        tail = p[:, 3 * GROUP:]
        wr = w0_ref[d] + _dot(jnp.tanh(tail), ww_ref[d])
        lw = -jnp.exp(-_softplus(-wr) - 0.5)
        a = _sigmoid(a0_ref[...] + _dot(tail, wa_ref[...]))
        kk = k * kk_ref[...]
        kk = kk * lax.rsqrt(_head_sum(kk * kk) + 1e-6)
        kmod = k * (1.0 + (a - 1.0) * ka_ref[...])
        bv = kk * a
        pc = _dot_sel_l(jnp.where(incl_t, 1.0, 0.0).astype(BF16), lw)
        pt = jnp.sum(lw, axis=0, keepdims=True)
        e_npc = jnp.exp(-pc)
        e_rest = jnp.exp(pt - pc)
        zr = jnp.concatenate([_stack4(-kk * jnp.exp(pc - lw), same), _stack4(r * jnp.exp(pc), same)],
                             axis=0).astype(BF16)
        prep[d, c] = dict(
            same=same, incl4=incl4, strict4=strict4, zr=zr,
            bt=_tile4(bv * e_npc).astype(BF16), kt=_tile4(kmod * e_npc).astype(BF16),
            v4=_stack4(v, same).astype(BF16),
            uvb=jnp.concatenate([_tile4(bv * e_rest), _tile4(kmod * e_rest)], axis=0).astype(BF16),
            dec=jnp.exp(pt), eye=jnp.where(diag4, 1.0, 0.0))
        yield
    keys = list(prep)
    ab = {key: _dot_nt(p['zr'], p['bt']) for key, p in prep.items()}
    ak = {key: _dot_nt(p['zr'], p['kt']) for key, p in prep.items()}
    yield
    ts = [jnp.where(prep[key]['strict4'], ab[key][:HW], 0.0) for key in keys]
    a_zk = {key: jnp.where(p['strict4'], ak[key][:HW], 0.0).astype(BF16) for key, p in prep.items()}
    a_rb = {key: jnp.where(p['incl4'], ab[key][HW:], 0.0).astype(BF16) for key, p in prep.items()}
    a_rk = {key: jnp.where(p['incl4'], ak[key][HW:], 0.0).astype(BF16) for key, p in prep.items()}
    zkv = {key: _dot(a_zk[key], p['v4']) for key, p in prep.items()}
    rkv = {key: _dot(a_rk[key], p['v4']) for key, p in prep.items()}
    yield from _neumann_stages(ts, prep[keys[0]]['eye'])
    tinv = {key: t.astype(BF16) for key, t in zip(keys, ts)}
    wu = {key: _dot(tinv[key], p['zr'][:HW]).astype(BF16) for key, p in prep.items()}
    uv = {key: _dot(tinv[key], zkv[key].astype(BF16)) for key in keys}
    yield
    st = {d: s_scr[d] for d, _ in keys}
    ub = {}
    for (d, c), p in prep.items():
        stb = st[d].astype(BF16)
        ub[d] = (_dot_nt(wu[d, c], stb) + uv[d, c]).astype(BF16)
        o_refs[d][c * CHUNK:(c + 1) * CHUNK, :] = _unstack4(
            _dot_nt(p['zr'][HW:], stb) + _dot(a_rb[d, c], ub[d]) + rkv[d, c])
    yield
    for (d, c), p in prep.items():
        upd = _dot_tn(jnp.concatenate([ub[d], p['v4']], axis=0), p['uvb'])
        s_scr[d] = st[d] * p['dec'] + jnp.where(p['same'], upd, 0.0)


def _gdn_stages(wave, data, eb_ref, ea_ref, dtb_ref, alog_ref, s_scr, o_refs):
    prep = {}
    for d, c in _wave(wave):
        tail_ref, q_ref, k_ref, v_ref = data[d]
        same, _, _, diag4 = _stacked_masks(d)
        incl_t, strict_t = _token_masks(d, GROUP)
        incl_c, _ = _token_masks(d, CHUNK)
        incl_f = jnp.where(incl_c, 1.0, 0.0)
        q = _rows(q_ref, c)
        q = q * lax.rsqrt(_head_sum(q * q) + 1e-6) * HEAD_DIM ** -0.5
        k = _rows(k_ref, c)
        k = k * lax.rsqrt(_head_sum(k * k) + 1e-6)
        v = _rows(v_ref, c)
        tail = _rows(tail_ref, c)
        beta = _sigmoid(_dot_sel_r(tail, eb_ref[d]))
        g = -jnp.exp(alog_ref[d]) * _softplus(_dot_sel_r(tail, ea_ref[d]) + dtb_ref[d])
        cc, rr = _cum_rows_cols(g, incl_f.astype(BF16), incl_f.T)
        seg = cc - rr
        gt = jnp.sum(g, axis=0, keepdims=True)
        e_cc = jnp.exp(cc)
        kb = k * beta
        prep[d, c] = dict(
            same=same,
            d_strict=_stack4(jnp.exp(jnp.where(strict_t, seg, -jnp.inf)), same),
            d_incl=_stack4(jnp.exp(jnp.where(incl_t, seg, -jnp.inf)), same),
            kb4=_stack4(kb, same).astype(BF16), kt=_tile4(k).astype(BF16),
            q4=_stack4(q, same).astype(BF16),
            rhs=jnp.concatenate([_stack4(kb * e_cc, same), _stack4(v * beta, same)], axis=1).astype(BF16),
            qd4=_stack4(q * e_cc, same).astype(BF16),
            kd4=_stack4(k * jnp.exp(gt - cc), same).astype(BF16),
            dec=jnp.exp(gt), eye=jnp.where(diag4, 1.0, 0.0))
        yield
    keys = list(prep)
    ts = [-(_dot_nt(prep[key]['kb4'], prep[key]['kt']) * prep[key]['d_strict']) for key in keys]
    a_qk = {key: (_dot_nt(p['q4'], p['kt']) * p['d_incl']).astype(BF16) for key, p in prep.items()}
    yield
    yield from _neumann_stages(ts, prep[keys[0]]['eye'])
    sol = {key: _dot(t.astype(BF16), prep[key]['rhs']) for key, t in zip(keys, ts)}
    yield
    st = {d: s_scr[d] for d, _ in keys}
    v_new = {}
    for d, c in keys:
        x = sol[d, c]
        v_new[d] = (x[:, HW:] - _dot(x[:, :HW].astype(BF16), st[d].astype(BF16))).astype(BF16)
    yield
    for (d, c), p in prep.items():
        o_refs[d][c * CHUNK:(c + 1) * CHUNK, :] = _unstack4(
            _dot(p['qd4'], st[d].astype(BF16)) + _dot(a_qk[d, c], v_new[d]))
        s_scr[d] = st[d] * p['dec'] + _dot_tn(p['kd4'], v_new[d])


WAVE_LAG = 3
N_SCAN_DATA = 11
N_SCAN_PARAMS = 15


def _scan_kernel(*refs):
    fwd = refs[:N_SCAN_DATA]
    bwd = refs[N_SCAN_DATA:2 * N_SCAN_DATA]
    params = refs[2 * N_SCAN_DATA:2 * N_SCAN_DATA + N_SCAN_PARAMS]
    rest = refs[2 * N_SCAN_DATA + N_SCAN_PARAMS:]
    inits, outs, finals, scr = rest[:4], rest[4:12], rest[12:16], rest[16:20]
    first = pl.program_id(1) == 0
    last = pl.program_id(1) == pl.num_programs(1) - 1

    @pl.when(first)
    def _():
        for s0_ref, s_scr in zip(inits, scr):
            s_scr[...] = s0_ref[...]

    both = lambda lo, hi: (fwd[lo:hi], bwd[lo:hi])
    (gla_w2, gla_b, rwkv_w0, rwkv_ww, rwkv_a0, rwkv_wa, rwkv_kk, rwkv_ka,
     ssd_e, ssd_dtb, ssd_alog, gdn_eb, gdn_ea, gdn_dtb, gdn_alog) = params
    def mixers(wave):
        return [
            _gdn_stages(wave, both(7, 11), gdn_eb, gdn_ea, gdn_dtb, gdn_alog, scr[3], outs[6:8]),
            _rwkv_stages(wave, both(2, 3), rwkv_w0, rwkv_ww, rwkv_a0, rwkv_wa, rwkv_kk, rwkv_ka, scr[1],
                         outs[2:4]),
            _gla_stages(wave, both(0, 2), gla_w2, gla_b, scr[0], outs[0:2]),
            _ssd_stages(wave, both(3, 7), ssd_e, ssd_dtb, ssd_alog, scr[2], outs[4:6]),
        ]

    live = [(k * WAVE_LAG, gen) for k in range(CHUNKS_PER_STEP) for gen in mixers(k)]
    rnd = 0
    while live:
        for entry in list(live):
            if rnd >= entry[0] and next(entry[1], live) is live:
                live.remove(entry)
        rnd += 1

    @pl.when(last)
    def _():
        for sfin_ref, s_scr in zip(finals, scr):
            sfin_ref[...] = s_scr[...]


def _token_mixers(proj, convs, mp, init):
    pa, pb, pc, pd = proj
    (xs, bm, cm), (dq, dk, dv) = convs
    tail_blk = 4 * GROUP // LANE
    data = [(pa, 3 * GROUP, 0), (pa, LANE, tail_blk), (pb, WB, 0),
            (pc, LANE, tail_blk), (xs, GROUP, 0), (bm, GROUP, 0), (cm, GROUP, 0),
            (pd, LANE, tail_blk), (dq, GROUP, 0), (dk, GROUP, 0), (dv, GROUP, 0)]
    params = [mp[k] for k in ('gla_w2', 'gla_b', 'rwkv_w0', 'rwkv_ww', 'rwkv_a0', 'rwkv_wa', 'rwkv_kk',
                              'rwkv_ka', 'ssd_e', 'ssd_dtb', 'ssd_alog', 'gdn_eb', 'gdn_ea', 'gdn_dtb',
                              'gdn_alog')]
    assert len(data) == N_SCAN_DATA and len(params) == N_SCAN_PARAMS
    bsz, seq = pa.shape[:2]
    n_steps = seq // STEP_ROWS
    in_specs, args = [], []
    for mirror in (False, True):
        for arr, width, blk in data:
            idx = (lambda b, n, blk=blk: (b, n_steps - 1 - n, blk)) if mirror else \
                  (lambda b, n, blk=blk: (b, n, blk))
            in_specs.append(pl.BlockSpec((None, STEP_ROWS, width), idx))
            args.append(arr)
    for arr in params:
        in_specs.append(pl.BlockSpec(arr.shape, lambda b, n, nd=arr.ndim: (0,) * nd))
        args.append(arr)
    st_specs = [pl.BlockSpec((None,) + s.shape[1:], lambda b, n, nd=s.ndim - 1: (b,) + (0,) * nd)
                for s in init]
    in_specs += st_specs
    args += list(init)
    o_shape = jax.ShapeDtypeStruct((bsz, seq, GROUP), F32)
    o_specs = [pl.BlockSpec((None, STEP_ROWS, GROUP), lambda b, n: (b, n, 0)),
               pl.BlockSpec((None, STEP_ROWS, GROUP), lambda b, n: (b, n_steps - 1 - n, 0))] * 4
    res = pl.pallas_call(
        _scan_kernel,
        grid=(bsz, n_steps),
        in_specs=in_specs,
        out_specs=o_specs + st_specs,
        out_shape=[o_shape] * 8 + [jax.ShapeDtypeStruct(s.shape, F32) for s in init],
        scratch_shapes=[pltpu.VMEM(s.shape[1:], F32) for s in init],
        compiler_params=_params("parallel", "arbitrary"),
        name="mixer_scans",
    )(*args)
    return tuple(res[:8]), tuple(res[8:])


def _expand_matrix(row0):
    d = jnp.arange(2)[:, None, None]
    r = jnp.arange(LANE)[None, :, None]
    c = jnp.arange(GROUP)[None, None, :]
    return (r == row0 + N_HEADS * d + c // HEAD_DIM).astype(BF16)


def _per_head_lanes(t):
    return jnp.repeat(t.astype(F32), HEAD_DIM, axis=-1)[:, None, :]


def _rows_at(w, row0, total):
    pad = [(0, 0)] * (w.ndim - 2) + [(row0, total - row0 - w.shape[-2]), (0, 0)]
    return jnp.pad(w.astype(F32), pad)


def _mixer_params(lp):
    f = lambda t: t.astype(F32).reshape(1, GROUP)
    return dict(
        gla_w2=jnp.stack([_rows_at(lp['gla_gate_w2'][0], 0, LANE), _rows_at(lp['gla_gate_w2'][1], 16, LANE)]),
        gla_b=lp['gla_gate_b'].astype(F32)[:, None, :],
        rwkv_w0=lp['rwkv_w0'].astype(F32)[:, None, :],
        rwkv_ww=jnp.stack([_rows_at(lp['rwkv_w_w2'][0], 0, GROUP), _rows_at(lp['rwkv_w_w2'][1], 32, GROUP)]),
        rwkv_a0=f(lp['rwkv_a0']),
        rwkv_wa=_rows_at(lp['rwkv_a_w2'], 64, GROUP),
        rwkv_wg=_rows_at(lp['rwkv_g_w2'], 96, GROUP),
        rwkv_kk=f(lp['rwkv_k_k']),
        rwkv_ka=f(lp['rwkv_k_a']),
        ssd_e=_expand_matrix(0),
        ssd_dtb=_per_head_lanes(lp['ssm_dt_bias']),
        ssd_alog=_per_head_lanes(lp['ssm_A_log']),
        gdn_eb=_expand_matrix(0),
        gdn_ea=_expand_matrix(8),
        gdn_dtb=_per_head_lanes(lp['dn_dt_bias']),
        gdn_alog=_per_head_lanes(lp['dn_A_log']),
    )


V_GLA_G, V_A0, V_KA, V_RK, V_LN_G, V_LN_B, V_SSM_D, V_SSM_G, V_DN_G = range(9)


def _group_ones():
    r = lax.broadcasted_iota(jnp.int32, (GROUP, GROUP), 0) // SSM_STATE
    c = lax.broadcasted_iota(jnp.int32, (GROUP, GROUP), 1) // SSM_STATE
    return jnp.where(r == c, 1.0, 0.0).astype(BF16)


def _outproj_kernel(x_ref, ga_ref, oaf_ref, oab_ref, obf_ref, obb_ref, ocf_ref, ocb_ref, odf_ref, odb_ref,
                    ag_ref, pb_ref, cz_ref, xs_ref, dz_ref, vec_ref, wa_ref, wg_ref, wout_ref, o_ref):
    vec = lambda i: vec_ref[i:i + 1, :]
    inv_hd = 1.0 / HEAD_DIM

    o = oaf_ref[...] + oab_ref[...]
    ya = o * lax.rsqrt(_head_stat(o * o) * inv_hd + RMS_EPS) * vec(V_GLA_G) * _silu(ag_ref[...])

    pb = pb_ref[...]
    r = pb[:, :GROUP]
    k = pb[:, GROUP:2 * GROUP]
    v = pb[:, 2 * GROUP:3 * GROUP]
    tail = pb[:, 3 * GROUP:]
    a = _sigmoid(vec(V_A0) + _dot(tail, wa_ref[...]))
    kmod = k * (1.0 + (a - 1.0) * vec(V_KA))
    gate = _dot(_sigmoid(tail), wg_ref[...])
    o = obf_ref[...] + obb_ref[...]
    dev = o - _head_stat(o) * inv_hd
    on = dev * lax.rsqrt(_head_stat(dev * dev) * inv_hd + RWKV_GN_EPS) * vec(V_LN_G) + vec(V_LN_B)
    yb = (on + _head_stat(r * kmod * vec(V_RK)) * v) * gate

    y = (ocf_ref[...] + ocb_ref[...] + vec(V_SSM_D) * xs_ref[...]) * _silu(cz_ref[...])
    yc = y * lax.rsqrt(_dot_stat(y * y, _group_ones()) * (1.0 / SSM_STATE) + RMS_EPS) * vec(V_SSM_G)

    o = odf_ref[...] + odb_ref[...]
    yd = o * lax.rsqrt(_head_stat(o * o) * inv_hd + RMS_EPS) * vec(V_DN_G) * _silu(dz_ref[...])

    cat = jnp.concatenate([ya, yb, yc, yd], axis=1).astype(BF16)
    o_ref[...] = x_ref[...] + ga_ref[...] * _dot(cat, wout_ref[...])


def _out_projection(x, gate, outs, proj, xs, vecs, wa, wg, w_out):
    bsz, seq, d = x.shape
    tm = 256
    pa, pb, pc, pd = proj
    col = lambda blk: pl.BlockSpec((None, tm, GROUP), lambda b, i: (b, i, blk))
    full = lambda arr: pl.BlockSpec(arr.shape, lambda b, i: (0, 0))
    return pl.pallas_call(
        _outproj_kernel,
        grid=(bsz, seq // tm),
        in_specs=[pl.BlockSpec((None, tm, d), lambda b, i: (b, i, 0)),
                  pl.BlockSpec((None, 1, d), lambda b, i: (b, 0, 0)),
                  col(0), col(0), col(0), col(0), col(0), col(0), col(0), col(0),
                  col(3),
                  pl.BlockSpec((None, tm, WB), lambda b, i: (b, i, 0)),
                  col(0), col(0), col(3),
                  full(vecs), full(wa), full(wg), full(w_out)],
        out_specs=pl.BlockSpec((None, tm, d), lambda b, i: (b, i, 0)),
        out_shape=jax.ShapeDtypeStruct(x.shape, F32),
        compiler_params=_params("parallel", "parallel"),
        name="out_projection",
    )(x, gate, *outs, pa, pb, pc, xs, pd, vecs, wa, wg, w_out)


def _route_kernel(x_ref, g_ref, sh_ref, sc_ref, rw_ref, rb_ref, h_ref, wd_ref):
    h = _modulated_norm(x_ref[...], g_ref[...], sh_ref[...], sc_ref[...]).astype(BF16)
    h_ref[...] = h
    scores = _sigmoid(_dot(h, rw_ref[...]))
    sel = scores + rb_ref[...]
    lane = lax.broadcasted_iota(jnp.int32, sel.shape, 1)
    chosen = jnp.zeros(sel.shape, F32)
    for _ in range(TOP_K):
        m = jnp.max(sel, axis=-1, keepdims=True)
        first = jnp.min(jnp.where(sel == m, lane, N_EXPERTS), axis=-1, keepdims=True)
        pick = lane == first
        chosen = jnp.where(pick, 1.0, chosen)
        sel = jnp.where(pick, -jnp.inf, sel)
    w = scores * chosen
    wd_ref[...] = w / jnp.sum(w, axis=-1, keepdims=True) * ROUTED_SCALE


def _route(x, g, shift, scale, router_w, router_bias):
    bsz, seq, d = x.shape
    tm = 256
    vec = pl.BlockSpec((None, 1, d), lambda b, i: (b, 0, 0))
    return pl.pallas_call(
        _route_kernel,
        grid=(bsz, seq // tm),
        in_specs=[pl.BlockSpec((None, tm, d), lambda b, i: (b, i, 0)),
                  pl.BlockSpec((1, d), lambda b, i: (0, 0)),
                  vec, vec,
                  pl.BlockSpec((d, N_EXPERTS), lambda b, i: (0, 0)),
                  pl.BlockSpec((1, N_EXPERTS), lambda b, i: (0, 0))],
        out_specs=[pl.BlockSpec((None, tm, d), lambda b, i: (b, i, 0)),
                   pl.BlockSpec((None, tm, N_EXPERTS), lambda b, i: (b, i, 0))],
        out_shape=[jax.ShapeDtypeStruct((bsz, seq, d), BF16),
                   jax.ShapeDtypeStruct((bsz, seq, N_EXPERTS), F32)],
        compiler_params=_params("parallel", "parallel"),
        name="moe_route",
    )(x, g.reshape(1, d), shift, scale, router_w, router_bias.reshape(1, N_EXPERTS))


def _swiglu_hidden(h, w_gate, w_up):
    return _silu(_dot(h, w_gate)) * _dot(h, w_up)


def _experts_kernel(h_ref, wd_ref, x_ref, gf_ref, wg_ref, wu_ref, wdn_ref, sg_ref, su_ref, sdn_ref, fg_ref,
                    o_ref, acc_ref, *, final_norm):
    grp = pl.program_id(2)
    per_step = wg_ref.shape[0]
    h = h_ref[...]

    @pl.when(grp == 0)
    def _():
        acc_ref[...] = _dot(_swiglu_hidden(h, sg_ref[...], su_ref[...]).astype(BF16), sdn_ref[...])

    wd = wd_ref[...]
    lane = lax.broadcasted_iota(jnp.int32, wd.shape, 1)
    acts = []
    for j in range(per_step):
        w_e = jnp.sum(jnp.where(lane == grp * per_step + j, wd, 0.0), axis=-1, keepdims=True)
        acts.append((_swiglu_hidden(h, wg_ref[j], wu_ref[j]) * w_e).astype(BF16))
    w_down = wdn_ref[...].reshape(per_step * EXPERT_DIM, wdn_ref.shape[-1])
    acc_ref[...] += _dot(jnp.concatenate(acts, axis=1), w_down)

    @pl.when(grp == pl.num_programs(2) - 1)
    def _():
        y = x_ref[...] + gf_ref[...] * acc_ref[...]
        if final_norm:
            y = y * lax.rsqrt(jnp.mean(y * y, axis=-1, keepdims=True) + RMS_EPS) * fg_ref[...]
        o_ref[...] = y


EXPERTS_PER_STEP = 4


def _experts(h, wd, x, gate, layer, weights, final_g, final_norm):
    bsz, seq, d = x.shape
    tm = min(seq, 1024)
    eps = EXPERTS_PER_STEP
    w_gate, w_up, w_down, sh_gate, sh_up, sh_down = weights
    tok = lambda w: pl.BlockSpec((None, tm, w), lambda b, i, e: (b, i, 0))
    routed = lambda r, c: pl.BlockSpec((None, eps, r, c), lambda b, i, e: (layer, e, 0, 0))
    shared = lambda r, c: pl.BlockSpec((None, r, c), lambda b, i, e: (layer, 0, 0))
    return pl.pallas_call(
        functools.partial(_experts_kernel, final_norm=final_norm),
        grid=(bsz, seq // tm, N_EXPERTS // eps),
        in_specs=[tok(d), tok(N_EXPERTS), tok(d),
                  pl.BlockSpec((None, 1, d), lambda b, i, e: (b, 0, 0)),
                  routed(d, EXPERT_DIM), routed(d, EXPERT_DIM), routed(EXPERT_DIM, d),
                  shared(d, EXPERT_DIM), shared(d, EXPERT_DIM), shared(EXPERT_DIM, d),
                  pl.BlockSpec((1, d), lambda b, i, e: (0, 0))],
        out_specs=tok(d),
        out_shape=jax.ShapeDtypeStruct(x.shape, F32),
        scratch_shapes=[pltpu.VMEM((tm, d), F32)],
        compiler_params=_params("parallel", "parallel", "arbitrary"),
        name="moe_experts",
    )(h, wd, x, gate, w_gate, w_up, w_down, sh_gate, sh_up, sh_down, final_g.reshape(1, d))


def _pad_w_in(w):
    cuts = (0, 1056, 1984, 3016, 4056)
    widths = (WA, WB, WC, WD)
    parts = []
    for lo, hi, wd in zip(cuts[:-1], cuts[1:], widths):
        parts.append(jnp.pad(w[:, lo:hi], ((0, 0), (0, wd - (hi - lo)))))
    return jnp.concatenate(parts, axis=1).astype(BF16)


def kernel(x, c, ctx, c_ctx, norm1_g, norm2_g, w_mod, b_mod, w_in, w_out, gla_gate_w2, gla_gate_b, gla_norm_g, rwkv_w_w2, rwkv_w0, rwkv_a_w2, rwkv_a0, rwkv_g_w2, rwkv_k_k, rwkv_k_a, rwkv_r_k, rwkv_ln_g, rwkv_ln_b, ssm_conv_w, ssm_conv_b, ssm_A_log, ssm_dt_bias, ssm_D, ssm_norm_g, dn_conv_w, dn_A_log, dn_dt_bias, dn_norm_g, router_w, router_bias, exp_w_gate, exp_w_up, exp_w_down, sh_w_gate, sh_w_up, sh_w_down, final_norm_g):
    bsz, n_lat, d = x.shape
    n_ctx = ctx.shape[1]
    depth = w_in.shape[0]
    rows = n_lat // GRID_W
    x = x.astype(F32)
    ctx = ctx.astype(F32)
    c_rows = jnp.concatenate([c, c_ctx[None, :], jnp.zeros((8 - bsz - 1, d), c.dtype)], axis=0).astype(F32)
    zero_init = (jnp.zeros((bsz, 2, HW, HW), F32), jnp.zeros((bsz, 2, HW, HW), F32),
                 jnp.zeros((bsz, 2, HW, SSM_STATE), F32), jnp.zeros((bsz, 2, HW, HW), F32))
    moe_weights = tuple(w.astype(BF16) for w in (exp_w_gate, exp_w_up, exp_w_down,
                                                  sh_w_gate, sh_w_up, sh_w_down))
    for i in range(depth):
        last = i == depth - 1
        lp = dict(gla_gate_w2=gla_gate_w2[i], gla_gate_b=gla_gate_b[i], rwkv_w_w2=rwkv_w_w2[i],
                  rwkv_w0=rwkv_w0[i], rwkv_a_w2=rwkv_a_w2[i], rwkv_a0=rwkv_a0[i],
                  rwkv_g_w2=rwkv_g_w2[i], rwkv_k_k=rwkv_k_k[i], rwkv_k_a=rwkv_k_a[i],
                  ssm_A_log=ssm_A_log[i], ssm_dt_bias=ssm_dt_bias[i],
                  dn_A_log=dn_A_log[i], dn_dt_bias=dn_dt_bias[i])
        mp = _mixer_params(lp)
        vecs = jnp.stack([gla_norm_g[i], rwkv_a0[i], rwkv_k_a[i], rwkv_r_k[i].reshape(GROUP),
                          rwkv_ln_g[i], rwkv_ln_b[i], jnp.repeat(ssm_D[i], HEAD_DIM), ssm_norm_g[i],
                          dn_norm_g[i]] + [jnp.zeros((GROUP,), F32)] * 7).astype(F32)
        w_in_pad = _pad_w_in(w_in[i])
        w_out_b = w_out[i].astype(BF16)
        dn_conv_b = jnp.zeros((3 * GROUP,), F32)

        mod = _modulation(c_rows, w_mod[i].astype(F32), b_mod[i].astype(F32))
        mod_x = [m[:, None, :] for m in jnp.split(mod[:bsz], 6, axis=-1)]
        mod_c = [jnp.broadcast_to(m[None], (bsz, 1, d)) for m in jnp.split(mod[bsz:bsz + 1], 6, axis=-1)]

        def mix(tokens, mods, init, g_rows, g_cols):
            proj = _in_projection(tokens, norm1_g[i], mods[0], mods[1], w_in_pad)
            convs = (_grid_conv(proj[2], GROUP, ssm_conv_w[i], ssm_conv_b[i], g_rows, g_cols),
                     _grid_conv(proj[3], 0, dn_conv_w[i], dn_conv_b, g_rows, g_cols))
            outs, states = _token_mixers(proj, convs, mp, init)
            return proj, convs, outs, states

        proj_c, convs_c, outs_c, ctx_states = mix(ctx, mod_c, zero_init, 1, n_ctx)
        proj_x, convs_x, outs_x, _ = mix(x, mod_x, ctx_states, rows, GRID_W)
        x = _out_projection(x, mod_x[2], outs_x, proj_x, convs_x[0][0], vecs,
                            mp['rwkv_wa'], mp['rwkv_wg'], w_out_b)

        def ffn(tokens, mods, final, flatten=False):
            shape = tokens.shape
            if flatten:
                tokens = tokens.reshape(1, shape[0] * shape[1], d)
                mods = [m[:1] for m in mods]
            h, wd = _route(tokens, norm2_g[i], mods[3], mods[4], router_w[i].astype(BF16),
                           router_bias[i].astype(F32))
            out = _experts(h, wd, tokens, mods[5], i, moe_weights, final_norm_g.astype(F32), final)
            return out.reshape(shape)

        x = ffn(x, mod_x, last)
        if not last:
            ctx = _out_projection(ctx, mod_c[2], outs_c, proj_c, convs_c[0][0], vecs,
                                  mp['rwkv_wa'], mp['rwkv_wg'], w_out_b)
            ctx = ffn(ctx, mod_c, False, flatten=True)
    return x
```

```python
import functools

import jax
import jax.numpy as jnp
from jax import lax
from jax.experimental import pallas as pl
from jax.experimental.pallas import tpu as pltpu

F32 = jnp.float32
BF16 = jnp.bfloat16

D_MODEL = 1024
GROUP = 256
N_HEADS = 4
HEAD_DIM = 64
CHUNK = 64
GRID_W = 64
SSM_STATE = 128
N_EXPERTS = 64
TOP_K = 8
EXPERT_DIM = 256
ROUTED_SCALE = 2.5
RMS_EPS = 1e-6
RWKV_GN_EPS = 64e-5
GLA_GATE_NORM = 16.0
LANE = 128
CONV_HALO = 128
VMEM_LIMIT = 56 * 1024 * 1024

WA = 4 * GROUP + LANE
WB = 4 * GROUP
WC = 4 * GROUP + LANE
WD = 4 * GROUP + LANE
W_IN_PAD = WA + WB + WC + WD

NT = (((1,), (1,)), ((), ()))
TN = (((0,), (0,)), ((), ()))


def _dot(a, b):
    return jnp.dot(a, b, preferred_element_type=F32)


def _dot_nt(a, b):
    return lax.dot_general(a, b, NT, preferred_element_type=F32)


def _dot_tn(a, b):
    return lax.dot_general(a, b, TN, preferred_element_type=F32)


def _split3(x):
    hi = x.astype(BF16)
    r1 = x - hi.astype(F32)
    mid = r1.astype(BF16)
    lo = (r1 - mid.astype(F32)).astype(BF16)
    return hi, mid, lo


def _dot_sel_l(sel, x):
    hi, mid, lo = _split3(x)
    return _dot(sel, hi) + _dot(sel, mid) + _dot(sel, lo)


def _dot_sel_r(x, sel):
    hi, mid, lo = _split3(x)
    return _dot(hi, sel) + _dot(mid, sel) + _dot(lo, sel)


def _dot3(a, b):
    ah, am, _ = _split3(a)
    bh, bm, _ = _split3(b)
    return _dot(ah, bh) + (_dot(ah, bm) + _dot(am, bh))


def _softplus(x):
    return jnp.maximum(x, 0.0) + jnp.log1p(jnp.exp(-jnp.abs(x)))


def _sigmoid(x):
    return 1.0 / (1.0 + jnp.exp(-x))


def _silu(x):
    return x * _sigmoid(x)


def _head_ones():
    r = lax.broadcasted_iota(jnp.int32, (GROUP, GROUP), 0) // HEAD_DIM
    c = lax.broadcasted_iota(jnp.int32, (GROUP, GROUP), 1) // HEAD_DIM
    return jnp.where(r == c, 1.0, 0.0).astype(BF16)


def _dot_stat(x, sel):
    hi = x.astype(BF16)
    lo = (x - hi.astype(F32)).astype(BF16)
    return _dot(hi, sel) + _dot(lo, sel)


def _head_sum(x):
    return _dot_sel_r(x, _head_ones())


def _head_stat(x):
    return _dot_stat(x, _head_ones())


def _params(*sem):
    return pltpu.CompilerParams(dimension_semantics=sem, vmem_limit_bytes=VMEM_LIMIT)


def _mod_kernel(c_ref, w_ref, b_ref, o_ref):
    o_ref[...] = _dot3(_silu(c_ref[...]), w_ref[...]) + b_ref[...]


def _modulation(c_rows, w, b):
    m, d = c_rows.shape
    n = w.shape[1]
    tn = 1536
    return pl.pallas_call(
        _mod_kernel,
        grid=(n // tn,),
        in_specs=[pl.BlockSpec((m, d), lambda j: (0, 0)),
                  pl.BlockSpec((d, tn), lambda j: (0, j)),
                  pl.BlockSpec((1, tn), lambda j: (0, j))],
        out_specs=pl.BlockSpec((m, tn), lambda j: (0, j)),
        out_shape=jax.ShapeDtypeStruct((m, n), F32),
        compiler_params=_params("parallel"),
        name="modulation",
    )(c_rows, w, b.reshape(1, n))


def _modulated_norm(x, g, shift, scale):
    y = x * lax.rsqrt(jnp.mean(x * x, axis=-1, keepdims=True) + RMS_EPS)
    return y * g * (1.0 + scale) + shift


def _inproj_kernel(x_ref, g_ref, sh_ref, sc_ref, w_ref, oa_ref, ob_ref, oc_ref, od_ref):
    h = _modulated_norm(x_ref[...], g_ref[...], sh_ref[...], sc_ref[...]).astype(BF16)
    y = _dot(h, w_ref[...])
    oa_ref[...] = y[:, :WA]
    ob_ref[...] = y[:, WA:WA + WB]
    oc_ref[...] = y[:, WA + WB:WA + WB + WC]
    od_ref[...] = y[:, WA + WB + WC:]


def _in_projection(x, g, shift, scale, w_pad):
    bsz, seq, d = x.shape
    tm = 256
    vec = pl.BlockSpec((None, 1, d), lambda b, i: (b, 0, 0))
    out = lambda w: pl.BlockSpec((None, tm, w), lambda b, i: (b, i, 0))
    shp = lambda w: jax.ShapeDtypeStruct((bsz, seq, w), F32)
    return pl.pallas_call(
        _inproj_kernel,
        grid=(bsz, seq // tm),
        in_specs=[pl.BlockSpec((None, tm, d), lambda b, i: (b, i, 0)),
                  pl.BlockSpec((1, d), lambda b, i: (0, 0)),
                  vec, vec,
                  pl.BlockSpec((d, W_IN_PAD), lambda b, i: (0, 0))],
        out_specs=[out(WA), out(WB), out(WC), out(WD)],
        out_shape=[shp(WA), shp(WB), shp(WC), shp(WD)],
        compiler_params=_params("parallel", "parallel"),
        name="in_projection",
    )(x, g.reshape(1, d), shift, scale, w_pad)


def _conv_kernel(prev_ref, cur_ref, next_ref, w_ref, b_ref, o_ref, ext_ref, *, cols, taps_r, n_tiles):
    i = pl.program_id(1)
    tm, ch = cur_ref.shape
    ext_ref[CONV_HALO:CONV_HALO + tm, :] = cur_ref[...]
    ext_ref[:CONV_HALO, :] = jnp.where(i > 0, prev_ref[...], 0.0)
    ext_ref[CONV_HALO + tm:, :] = jnp.where(i < n_tiles - 1, next_ref[...], 0.0)
    rt = 128
    for r0 in range(0, tm, rt):
        col = (lax.broadcasted_iota(jnp.int32, (rt, LANE), 0) + r0) % cols
        inside = {-1: col >= 1, 1: col < cols - 1}
        for c0 in range(0, ch, LANE):
            acc = jnp.zeros((rt, LANE), F32) + b_ref[:, c0:c0 + LANE]
            for dc in (-1, 0, 1):
                part = jnp.zeros((rt, LANE), F32)
                for dr in taps_r:
                    off = CONV_HALO + r0 + dr * cols + dc
                    k = (dr + 1) * 3 + (dc + 1)
                    part = part + ext_ref[off:off + rt, c0:c0 + LANE] * w_ref[k:k + 1, c0:c0 + LANE]
                acc = acc + (part if dc == 0 else jnp.where(inside[dc], part, 0.0))
            o_ref[r0:r0 + rt, c0:c0 + LANE] = _silu(acc)


def _grid_conv(proj, lane0, w, b, rows, cols):
    bsz, seq, _ = proj.shape
    tm = min(seq, 512)
    n_tiles = seq // tm
    hpt = tm // CONV_HALO
    n_halo = seq // CONV_HALO
    reach = cols + 1 if rows > 1 else 1
    assert seq == rows * cols and tm % cols == 0 and reach <= CONV_HALO and lane0 % GROUP == 0
    kern = functools.partial(_conv_kernel, cols=cols, taps_r=(-1, 0, 1) if rows > 1 else (0,),
                             n_tiles=n_tiles)
    w9 = w.reshape(9, 3 * GROUP)
    b1 = b.reshape(1, 3 * GROUP)
    outs = []
    for j in range(3):
        lb = lane0 // GROUP + j
        outs.append(pl.pallas_call(
            kern,
            grid=(bsz, n_tiles),
            in_specs=[
                pl.BlockSpec((None, CONV_HALO, GROUP),
                             lambda bi, i, lb=lb: (bi, jnp.maximum(i * hpt - 1, 0), lb)),
                pl.BlockSpec((None, tm, GROUP), lambda bi, i, lb=lb: (bi, i, lb)),
                pl.BlockSpec((None, CONV_HALO, GROUP),
                             lambda bi, i, lb=lb: (bi, jnp.minimum((i + 1) * hpt, n_halo - 1), lb)),
                pl.BlockSpec((9, GROUP), lambda bi, i, j=j: (0, j)),
                pl.BlockSpec((1, GROUP), lambda bi, i, j=j: (0, j))],
            out_specs=pl.BlockSpec((None, tm, GROUP), lambda bi, i: (bi, i, 0)),
            out_shape=jax.ShapeDtypeStruct((bsz, seq, GROUP), F32),
            scratch_shapes=[pltpu.VMEM((tm + 2 * CONV_HALO, GROUP), F32)],
            compiler_params=_params("parallel", "parallel"),
            name="grid_conv",
        )(proj, proj, proj, w9, b1))
    return outs


HW = N_HEADS * CHUNK


def _token_masks(direction, width):
    r = lax.broadcasted_iota(jnp.int32, (CHUNK, width), 0)
    c = lax.broadcasted_iota(jnp.int32, (CHUNK, width), 1) % CHUNK
    diff = (r - c) * (1 - 2 * direction)
    return diff >= 0, diff > 0


def _stacked_masks(direction):
    r = lax.broadcasted_iota(jnp.int32, (HW, HW), 0)
    c = lax.broadcasted_iota(jnp.int32, (HW, HW), 1)
    same = (r // CHUNK) == (c // CHUNK)
    diff = (r - c) * (1 - 2 * direction)
    return same, same & (diff >= 0), same & (diff > 0), r == c


def _tile4(x):
    return jnp.concatenate([x] * N_HEADS, axis=0)


def _stack4(x, same):
    return jnp.where(same, _tile4(x), 0.0)


def _unstack4(y):
    return (y[:CHUNK] + y[CHUNK:2 * CHUNK]) + (y[2 * CHUNK:3 * CHUNK] + y[3 * CHUNK:])


def _neumann_stages(ns, eye):
    ts = [eye + n for n in ns]
    ps = [n.astype(BF16) for n in ns]
    for _ in range(5):
        ps = [_dot(p, p).astype(BF16) for p in ps]
        yield
        for j, (t, p) in enumerate(zip(ts, ps)):
            ts[j] = t + _dot(t.astype(BF16), p)
        yield
    ns[:] = ts


def _cum_rows_cols(g, incl_f, incl_t_f):
    cc = _dot_sel_l(incl_f, g)
    ones = jnp.ones((CHUNK, CHUNK), BF16)
    rr = _dot_sel_l(ones, g * jnp.concatenate([incl_t_f] * N_HEADS, axis=1))
    return cc, rr


DIRS = (0, 1)
CHUNKS_PER_STEP = 4
STEP_ROWS = CHUNKS_PER_STEP * CHUNK


def _visit_order(direction):
    order = tuple(range(CHUNKS_PER_STEP))
    return order if direction == 0 else order[::-1]


def _wave(k):
    return [(d, _visit_order(d)[k]) for d in DIRS]


def _rows(ref, c):
    return ref[c * CHUNK:(c + 1) * CHUNK, :]


def _gla_stages(wave, data, w2_ref, b_ref, s_scr, o_refs):
    prep = {}
    for d, c in _wave(wave):
        qkv_ref, tail_ref = data[d]
        same, incl4, _, _ = _stacked_masks(d)
        incl_t, _ = _token_masks(d, CHUNK)
        p = _rows(qkv_ref, c)
        q = p[:, :GROUP] * HEAD_DIM ** -0.5
        k = p[:, GROUP:2 * GROUP]
        v = p[:, 2 * GROUP:]
        zg = _dot(_rows(tail_ref, c), w2_ref[d]) + b_ref[d]
        gk = (jnp.minimum(zg, 0.0) - jnp.log1p(jnp.exp(-jnp.abs(zg)))) / GLA_GATE_NORM
        bc = _dot_sel_l(jnp.where(incl_t, 1.0, 0.0).astype(BF16), gk)
        bt = jnp.sum(gk, axis=0, keepdims=True)
        prep[d, c] = dict(
            same=same, incl4=incl4,
            qt4=_stack4(q * jnp.exp(bc), same).astype(BF16),
            kt=_tile4(k * jnp.exp(-bc)).astype(BF16),
            ks=_tile4(k * jnp.exp(bt - bc)).astype(BF16),
            v4=_stack4(v, same).astype(BF16), dec=jnp.exp(bt))
        yield
    att = {key: jnp.where(p['incl4'], _dot_nt(p['qt4'], p['kt']), 0.0).astype(BF16)
           for key, p in prep.items()}
    yield
    intra = {key: _dot(att[key], p['v4']) for key, p in prep.items()}
    upd = {key: jnp.where(p['same'], _dot_tn(p['v4'], p['ks']), 0.0) for key, p in prep.items()}
    yield
    for (d, c), p in prep.items():
        st = s_scr[d]
        o_refs[d][c * CHUNK:(c + 1) * CHUNK, :] = _unstack4(
            intra[d, c] + _dot_nt(p['qt4'], st.astype(BF16)))
        s_scr[d] = st * p['dec'] + upd[d, c]


def _ssd_stages(wave, data, e_ref, dtb_ref, alog_ref, s_scr, o_refs):
    prep = {}
    for d, c in _wave(wave):
        tail_ref, xs_ref, bm_ref, cm_ref = data[d]
        same, _, _, _ = _stacked_masks(d)
        incl_t, _ = _token_masks(d, GROUP)
        incl_c, _ = _token_masks(d, CHUNK)
        incl_f = jnp.where(incl_c, 1.0, 0.0)
        dt = _softplus(_dot_sel_r(_rows(tail_ref, c), e_ref[d]) + dtb_ref[d])
        a = -jnp.exp(alog_ref[d]) * dt
        x = _rows(xs_ref, c) * dt
        cc, rr = _cum_rows_cols(a, incl_f.astype(BF16), incl_f.T)
        at = jnp.sum(a, axis=0, keepdims=True)
        bm = _rows(bm_ref, c)
        cm = _rows(cm_ref, c)
        by_head = lambda t: jnp.concatenate(
            [t[:, (h // 2) * SSM_STATE:(h // 2 + 1) * SSM_STATE] for h in range(N_HEADS)], axis=0)
        prep[d, c] = dict(
            lm4=_stack4(jnp.exp(jnp.where(incl_t, cc - rr, -jnp.inf)), same),
            bm4=by_head(bm).astype(BF16), cm4=by_head(cm).astype(BF16),
            x4=_stack4(x, same).astype(BF16),
            xd4=_stack4(x * jnp.exp(at - cc), same).astype(BF16),
            e4=_stack4(jnp.exp(cc), same),
            dec4=jnp.concatenate(
                [jnp.broadcast_to(jnp.exp(at[:, h * HEAD_DIM:h * HEAD_DIM + 1]), (CHUNK, SSM_STATE))
                 for h in range(N_HEADS)], axis=0))
        yield
    cb = {key: (_dot_nt(p['cm4'], p['bm4']) * p['lm4']).astype(BF16) for key, p in prep.items()}
    yield
    intra = {key: _dot(cb[key], p['x4']) for key, p in prep.items()}
    upd = {key: _dot_tn(p['xd4'], p['bm4']) for key, p in prep.items()}
    yield
    for (d, c), p in prep.items():
        st = s_scr[d]
        o_refs[d][c * CHUNK:(c + 1) * CHUNK, :] = _unstack4(
            intra[d, c] + _dot_nt(p['cm4'], st.astype(BF16)) * p['e4'])
        s_scr[d] = st * p['dec4'] + upd[d, c]


def _rwkv_stages(wave, data, w0_ref, ww_ref, a0_ref, wa_ref, kk_ref, ka_ref, s_scr, o_refs):
    prep = {}
    for d, c in _wave(wave):
        (p_ref,) = data[d]
        same, incl4, strict4, diag4 = _stacked_masks(d)
        incl_t, _ = _token_masks(d, CHUNK)
        p = _rows(p_ref, c)
        r = p[:, :GROUP]
        k = p[:, GROUP:2 * GROUP]
        v = p[:, 2 * GROUP:3 * GROUP]
        tail = p[:, 3 * GROUP:]
        wr = w0_ref[d] + _dot(jnp.tanh(tail), ww_ref[d])
        lw = -jnp.exp(-_softplus(-wr) - 0.5)
        a = _sigmoid(a0_ref[...] + _dot(tail, wa_ref[...]))
        kk = k * kk_ref[...]
        kk = kk * lax.rsqrt(_head_sum(kk * kk) + 1e-6)
        kmod = k * (1.0 + (a - 1.0) * ka_ref[...])
        bv = kk * a
        pc = _dot_sel_l(jnp.where(incl_t, 1.0, 0.0).astype(BF16), lw)
        pt = jnp.sum(lw, axis=0, keepdims=True)
        e_npc = jnp.exp(-pc)
        e_rest = jnp.exp(pt - pc)
        zr = jnp.concatenate([_stack4(-kk * jnp.exp(pc - lw), same), _stack4(r * jnp.exp(pc), same)],
                             axis=0).astype(BF16)
        prep[d, c] = dict(
            same=same, incl4=incl4, strict4=strict4, zr=zr,
            bt=_tile4(bv * e_npc).astype(BF16), kt=_tile4(kmod * e_npc).astype(BF16),
            v4=_stack4(v, same).astype(BF16),
            uvb=jnp.concatenate([_tile4(bv * e_rest), _tile4(kmod * e_rest)], axis=0).astype(BF16),
            dec=jnp.exp(pt), eye=jnp.where(diag4, 1.0, 0.0))
        yield
    keys = list(prep)
    ab = {key: _dot_nt(p['zr'], p['bt']) for key, p in prep.items()}
    ak = {key: _dot_nt(p['zr'], p['kt']) for key, p in prep.items()}
    yield
    ts = [jnp.where(prep[key]['strict4'], ab[key][:HW], 0.0) for key in keys]
    a_zk = {key: jnp.where(p['strict4'], ak[key][:HW], 0.0).astype(BF16) for key, p in prep.items()}
    a_rb = {key: jnp.where(p['incl4'], ab[key][HW:], 0.0).astype(BF16) for key, p in prep.items()}
    a_rk = {key: jnp.where(p['incl4'], ak[key][HW:], 0.0).astype(BF16) for key, p in prep.items()}
    zkv = {key: _dot(a_zk[key], p['v4']) for key, p in prep.items()}
    rkv = {key: _dot(a_rk[key], p['v4']) for key, p in prep.items()}
    yield from _neumann_stages(ts, prep[keys[0]]['eye'])
    tinv = {key: t.astype(BF16) for key, t in zip(keys, ts)}
    wu = {key: _dot(tinv[key], p['zr'][:HW]).astype(BF16) for key, p in prep.items()}
    uv = {key: _dot(tinv[key], zkv[key].astype(BF16)) for key in keys}
    yield
    st = {d: s_scr[d] for d, _ in keys}
    ub = {}
    for (d, c), p in prep.items():
        stb = st[d].astype(BF16)
        ub[d] = (_dot_nt(wu[d, c], stb) + uv[d, c]).astype(BF16)
        o_refs[d][c * CHUNK:(c + 1) * CHUNK, :] = _unstack4(
            _dot_nt(p['zr'][HW:], stb) + _dot(a_rb[d, c], ub[d]) + rkv[d, c])
    yield
    for (d, c), p in prep.items():
        upd = _dot_tn(jnp.concatenate([ub[d], p['v4']], axis=0), p['uvb'])
        s_scr[d] = st[d] * p['dec'] + jnp.where(p['same'], upd, 0.0)


def _gdn_stages(wave, data, eb_ref, ea_ref, dtb_ref, alog_ref, s_scr, o_refs):
    prep = {}
    for d, c in _wave(wave):
        tail_ref, q_ref, k_ref, v_ref = data[d]
        same, _, _, diag4 = _stacked_masks(d)
        incl_t, strict_t = _token_masks(d, GROUP)
        incl_c, _ = _token_masks(d, CHUNK)
        incl_f = jnp.where(incl_c, 1.0, 0.0)
        q = _rows(q_ref, c)
        q = q * lax.rsqrt(_head_sum(q * q) + 1e-6) * HEAD_DIM ** -0.5
        k = _rows(k_ref, c)
        k = k * lax.rsqrt(_head_sum(k * k) + 1e-6)
        v = _rows(v_ref, c)
        tail = _rows(tail_ref, c)
        beta = _sigmoid(_dot_sel_r(tail, eb_ref[d]))
        g = -jnp.exp(alog_ref[d]) * _softplus(_dot_sel_r(tail, ea_ref[d]) + dtb_ref[d])
        cc, rr = _cum_rows_cols(g, incl_f.astype(BF16), incl_f.T)
        seg = cc - rr
        gt = jnp.sum(g, axis=0, keepdims=True)
        e_cc = jnp.exp(cc)
        kb = k * beta
        prep[d, c] = dict(
            same=same,
            d_strict=_stack4(jnp.exp(jnp.where(strict_t, seg, -jnp.inf)), same),
            d_incl=_stack4(jnp.exp(jnp.where(incl_t, seg, -jnp.inf)), same),
            kb4=_stack4(kb, same).astype(BF16), kt=_tile4(k).astype(BF16),
            q4=_stack4(q, same).astype(BF16),
            rhs=jnp.concatenate([_stack4(kb * e_cc, same), _stack4(v * beta, same)], axis=1).astype(BF16),
            qd4=_stack4(q * e_cc, same).astype(BF16),
            kd4=_stack4(k * jnp.exp(gt - cc), same).astype(BF16),
            dec=jnp.exp(gt), eye=jnp.where(diag4, 1.0, 0.0))
        yield
    keys = list(prep)
    ts = [-(_dot_nt(prep[key]['kb4'], prep[key]['kt']) * prep[key]['d_strict']) for key in keys]
    a_qk = {key: (_dot_nt(p['q4'], p['kt']) * p['d_incl']).astype(BF16) for key, p in prep.items()}
    yield
    yield from _neumann_stages(ts, prep[keys[0]]['eye'])
    sol = {key: _dot(t.astype(BF16), prep[key]['rhs']) for key, t in zip(keys, ts)}
    yield
    st = {d: s_scr[d] for d, _ in keys}
    v_new = {}
    for d, c in keys:
        x = sol[d, c]
        v_new[d] = (x[:, HW:] - _dot(x[:, :HW].astype(BF16), st[d].astype(BF16))).astype(BF16)
    yield
    for (d, c), p in prep.items():
        o_refs[d][c * CHUNK:(c + 1) * CHUNK, :] = _unstack4(
            _dot(p['qd4'], st[d].astype(BF16)) + _dot(a_qk[d, c], v_new[d]))
        s_scr[d] = st[d] * p['dec'] + _dot_tn(p['kd4'], v_new[d])


WAVE_LAG = 3
N_SCAN_DATA = 11
N_SCAN_PARAMS = 15


def _scan_kernel(*refs):
    fwd = refs[:N_SCAN_DATA]
    bwd = refs[N_SCAN_DATA:2 * N_SCAN_DATA]
    params = refs[2 * N_SCAN_DATA:2 * N_SCAN_DATA + N_SCAN_PARAMS]
    rest = refs[2 * N_SCAN_DATA + N_SCAN_PARAMS:]
    inits, outs, finals, scr = rest[:4], rest[4:12], rest[12:16], rest[16:20]
    first = pl.program_id(1) == 0
    last = pl.program_id(1) == pl.num_programs(1) - 1

    @pl.when(first)
    def _():
        for s0_ref, s_scr in zip(inits, scr):
            s_scr[...] = s0_ref[...]

    both = lambda lo, hi: (fwd[lo:hi], bwd[lo:hi])
    (gla_w2, gla_b, rwkv_w0, rwkv_ww, rwkv_a0, rwkv_wa, rwkv_kk, rwkv_ka,
     ssd_e, ssd_dtb, ssd_alog, gdn_eb, gdn_ea, gdn_dtb, gdn_alog) = params
    def mixers(wave):
        return [
            _gdn_stages(wave, both(7, 11), gdn_eb, gdn_ea, gdn_dtb, gdn_alog, scr[3], outs[6:8]),
            _rwkv_stages(wave, both(2, 3), rwkv_w0, rwkv_ww, rwkv_a0, rwkv_wa, rwkv_kk, rwkv_ka, scr[1],
                         outs[2:4]),
            _gla_stages(wave, both(0, 2), gla_w2, gla_b, scr[0], outs[0:2]),
            _ssd_stages(wave, both(3, 7), ssd_e, ssd_dtb, ssd_alog, scr[2], outs[4:6]),
        ]

    live = [(k * WAVE_LAG, gen) for k in range(CHUNKS_PER_STEP) for gen in mixers(k)]
    rnd = 0
    while live:
        for entry in list(live):
            if rnd >= entry[0] and next(entry[1], live) is live:
                live.remove(entry)
        rnd += 1

    @pl.when(last)
    def _():
        for sfin_ref, s_scr in zip(finals, scr):
            sfin_ref[...] = s_scr[...]


def _token_mixers(proj, convs, mp, init):
    pa, pb, pc, pd = proj
    (xs, bm, cm), (dq, dk, dv) = convs
    tail_blk = 4 * GROUP // LANE
    data = [(pa, 3 * GROUP, 0), (pa, LANE, tail_blk), (pb, WB, 0),
            (pc, LANE, tail_blk), (xs, GROUP, 0), (bm, GROUP, 0), (cm, GROUP, 0),
            (pd, LANE, tail_blk), (dq, GROUP, 0), (dk, GROUP, 0), (dv, GROUP, 0)]
    params = [mp[k] for k in ('gla_w2', 'gla_b', 'rwkv_w0', 'rwkv_ww', 'rwkv_a0', 'rwkv_wa', 'rwkv_kk',
                              'rwkv_ka', 'ssd_e', 'ssd_dtb', 'ssd_alog', 'gdn_eb', 'gdn_ea', 'gdn_dtb',
                              'gdn_alog')]
    assert len(data) == N_SCAN_DATA and len(params) == N_SCAN_PARAMS
    bsz, seq = pa.shape[:2]
    n_steps = seq // STEP_ROWS
    in_specs, args = [], []
    for mirror in (False, True):
        for arr, width, blk in data:
            idx = (lambda b, n, blk=blk: (b, n_steps - 1 - n, blk)) if mirror else \
                  (lambda b, n, blk=blk: (b, n, blk))
            in_specs.append(pl.BlockSpec((None, STEP_ROWS, width), idx))
            args.append(arr)
    for arr in params:
        in_specs.append(pl.BlockSpec(arr.shape, lambda b, n, nd=arr.ndim: (0,) * nd))
        args.append(arr)
    st_specs = [pl.BlockSpec((None,) + s.shape[1:], lambda b, n, nd=s.ndim - 1: (b,) + (0,) * nd)
                for s in init]
    in_specs += st_specs
    args += list(init)
    o_shape = jax.ShapeDtypeStruct((bsz, seq, GROUP), F32)
    o_specs = [pl.BlockSpec((None, STEP_ROWS, GROUP), lambda b, n: (b, n, 0)),
               pl.BlockSpec((None, STEP_ROWS, GROUP), lambda b, n: (b, n_steps - 1 - n, 0))] * 4
    res = pl.pallas_call(
        _scan_kernel,
        grid=(bsz, n_steps),
        in_specs=in_specs,
        out_specs=o_specs + st_specs,
        out_shape=[o_shape] * 8 + [jax.ShapeDtypeStruct(s.shape, F32) for s in init],
        scratch_shapes=[pltpu.VMEM(s.shape[1:], F32) for s in init],
        compiler_params=_params("parallel", "arbitrary"),
        name="mixer_scans",
    )(*args)
    return tuple(res[:8]), tuple(res[8:])


def _expand_matrix(row0):
    d = jnp.arange(2)[:, None, None]
    r = jnp.arange(LANE)[None, :, None]
    c = jnp.arange(GROUP)[None, None, :]
    return (r == row0 + N_HEADS * d + c // HEAD_DIM).astype(BF16)


def _per_head_lanes(t):
    return jnp.repeat(t.astype(F32), HEAD_DIM, axis=-1)[:, None, :]


def _rows_at(w, row0, total):
    pad = [(0, 0)] * (w.ndim - 2) + [(row0, total - row0 - w.shape[-2]), (0, 0)]
    return jnp.pad(w.astype(F32), pad)


def _mixer_params(lp):
    f = lambda t: t.astype(F32).reshape(1, GROUP)
    return dict(
        gla_w2=jnp.stack([_rows_at(lp['gla_gate_w2'][0], 0, LANE), _rows_at(lp['gla_gate_w2'][1], 16, LANE)]),
        gla_b=lp['gla_gate_b'].astype(F32)[:, None, :],
        rwkv_w0=lp['rwkv_w0'].astype(F32)[:, None, :],
        rwkv_ww=jnp.stack([_rows_at(lp['rwkv_w_w2'][0], 0, GROUP), _rows_at(lp['rwkv_w_w2'][1], 32, GROUP)]),
        rwkv_a0=f(lp['rwkv_a0']),
        rwkv_wa=_rows_at(lp['rwkv_a_w2'], 64, GROUP),
        rwkv_wg=_rows_at(lp['rwkv_g_w2'], 96, GROUP),
        rwkv_kk=f(lp['rwkv_k_k']),
        rwkv_ka=f(lp['rwkv_k_a']),
        ssd_e=_expand_matrix(0),
        ssd_dtb=_per_head_lanes(lp['ssm_dt_bias']),
        ssd_alog=_per_head_lanes(lp['ssm_A_log']),
        gdn_eb=_expand_matrix(0),
        gdn_ea=_expand_matrix(8),
        gdn_dtb=_per_head_lanes(lp['dn_dt_bias']),
        gdn_alog=_per_head_lanes(lp['dn_A_log']),
    )


V_GLA_G, V_A0, V_KA, V_RK, V_LN_G, V_LN_B, V_SSM_D, V_SSM_G, V_DN_G = range(9)


def _group_ones():
    r = lax.broadcasted_iota(jnp.int32, (GROUP, GROUP), 0) // SSM_STATE
    c = lax.broadcasted_iota(jnp.int32, (GROUP, GROUP), 1) // SSM_STATE
    return jnp.where(r == c, 1.0, 0.0).astype(BF16)


def _outproj_kernel(x_ref, ga_ref, oaf_ref, oab_ref, obf_ref, obb_ref, ocf_ref, ocb_ref, odf_ref, odb_ref,
                    ag_ref, pb_ref, cz_ref, xs_ref, dz_ref, vec_ref, wa_ref, wg_ref, wout_ref, o_ref):
    vec = lambda i: vec_ref[i:i + 1, :]
    inv_hd = 1.0 / HEAD_DIM

    o = oaf_ref[...] + oab_ref[...]
    ya = o * lax.rsqrt(_head_stat(o * o) * inv_hd + RMS_EPS) * vec(V_GLA_G) * _silu(ag_ref[...])

    pb = pb_ref[...]
    r = pb[:, :GROUP]
    k = pb[:, GROUP:2 * GROUP]
    v = pb[:, 2 * GROUP:3 * GROUP]
    tail = pb[:, 3 * GROUP:]
    a = _sigmoid(vec(V_A0) + _dot(tail, wa_ref[...]))
    kmod = k * (1.0 + (a - 1.0) * vec(V_KA))
    gate = _dot(_sigmoid(tail), wg_ref[...])
    o = obf_ref[...] + obb_ref[...]
    dev = o - _head_stat(o) * inv_hd
    on = dev * lax.rsqrt(_head_stat(dev * dev) * inv_hd + RWKV_GN_EPS) * vec(V_LN_G) + vec(V_LN_B)
    yb = (on + _head_stat(r * kmod * vec(V_RK)) * v) * gate

    y = (ocf_ref[...] + ocb_ref[...] + vec(V_SSM_D) * xs_ref[...]) * _silu(cz_ref[...])
    yc = y * lax.rsqrt(_dot_stat(y * y, _group_ones()) * (1.0 / SSM_STATE) + RMS_EPS) * vec(V_SSM_G)

    o = odf_ref[...] + odb_ref[...]
    yd = o * lax.rsqrt(_head_stat(o * o) * inv_hd + RMS_EPS) * vec(V_DN_G) * _silu(dz_ref[...])

    cat = jnp.concatenate([ya, yb, yc, yd], axis=1).astype(BF16)
    o_ref[...] = x_ref[...] + ga_ref[...] * _dot(cat, wout_ref[...])


def _out_projection(x, gate, outs, proj, xs, vecs, wa, wg, w_out):
    bsz, seq, d = x.shape
    tm = 256
    pa, pb, pc, pd = proj
    col = lambda blk: pl.BlockSpec((None, tm, GROUP), lambda b, i: (b, i, blk))
    full = lambda arr: pl.BlockSpec(arr.shape, lambda b, i: (0, 0))
    return pl.pallas_call(
        _outproj_kernel,
        grid=(bsz, seq // tm),
        in_specs=[pl.BlockSpec((None, tm, d), lambda b, i: (b, i, 0)),
                  pl.BlockSpec((None, 1, d), lambda b, i: (b, 0, 0)),
                  col(0), col(0), col(0), col(0), col(0), col(0), col(0), col(0),
                  col(3),
                  pl.BlockSpec((None, tm, WB), lambda b, i: (b, i, 0)),
                  col(0), col(0), col(3),
                  full(vecs), full(wa), full(wg), full(w_out)],
        out_specs=pl.BlockSpec((None, tm, d), lambda b, i: (b, i, 0)),
        out_shape=jax.ShapeDtypeStruct(x.shape, F32),
        compiler_params=_params("parallel", "parallel"),
        name="out_projection",
    )(x, gate, *outs, pa, pb, pc, xs, pd, vecs, wa, wg, w_out)


def _route_kernel(x_ref, g_ref, sh_ref, sc_ref, rw_ref, rb_ref, h_ref, wd_ref):
    h = _modulated_norm(x_ref[...], g_ref[...], sh_ref[...], sc_ref[...]).astype(BF16)
    h_ref[...] = h
    scores = _sigmoid(_dot(h, rw_ref[...]))
    sel = scores + rb_ref[...]
    lane = lax.broadcasted_iota(jnp.int32, sel.shape, 1)
    chosen = jnp.zeros(sel.shape, F32)
    for _ in range(TOP_K):
        m = jnp.max(sel, axis=-1, keepdims=True)
        first = jnp.min(jnp.where(sel == m, lane, N_EXPERTS), axis=-1, keepdims=True)
        pick = lane == first
        chosen = jnp.where(pick, 1.0, chosen)
        sel = jnp.where(pick, -jnp.inf, sel)
    w = scores * chosen
    wd_ref[...] = w / jnp.sum(w, axis=-1, keepdims=True) * ROUTED_SCALE


def _route(x, g, shift, scale, router_w, router_bias):
    bsz, seq, d = x.shape
    tm = 256
    vec = pl.BlockSpec((None, 1, d), lambda b, i: (b, 0, 0))
    return pl.pallas_call(
        _route_kernel,
        grid=(bsz, seq // tm),
        in_specs=[pl.BlockSpec((None, tm, d), lambda b, i: (b, i, 0)),
                  pl.BlockSpec((1, d), lambda b, i: (0, 0)),
                  vec, vec,
                  pl.BlockSpec((d, N_EXPERTS), lambda b, i: (0, 0)),
                  pl.BlockSpec((1, N_EXPERTS), lambda b, i: (0, 0))],
        out_specs=[pl.BlockSpec((None, tm, d), lambda b, i: (b, i, 0)),
                   pl.BlockSpec((None, tm, N_EXPERTS), lambda b, i: (b, i, 0))],
        out_shape=[jax.ShapeDtypeStruct((bsz, seq, d), BF16),
                   jax.ShapeDtypeStruct((bsz, seq, N_EXPERTS), F32)],
        compiler_params=_params("parallel", "parallel"),
        name="moe_route",
    )(x, g.reshape(1, d), shift, scale, router_w, router_bias.reshape(1, N_EXPERTS))


def _swiglu_hidden(h, w_gate, w_up):
    return _silu(_dot(h, w_gate)) * _dot(h, w_up)


def _experts_kernel(h_ref, wd_ref, x_ref, gf_ref, wg_ref, wu_ref, wdn_ref, sg_ref, su_ref, sdn_ref, fg_ref,
                    o_ref, *, final_norm):
    acc_ref = o_ref
    grp = pl.program_id(2)
    per_step = wg_ref.shape[0]
    h = h_ref[...]

    @pl.when(grp == 0)
    def _():
        acc_ref[...] = _dot(_swiglu_hidden(h, sg_ref[...], su_ref[...]).astype(BF16), sdn_ref[...])

    wd = wd_ref[...]
    lane = lax.broadcasted_iota(jnp.int32, wd.shape, 1)
    acts = []
    for j in range(per_step):
        w_e = jnp.sum(jnp.where(lane == grp * per_step + j, wd, 0.0), axis=-1, keepdims=True)
        acts.append((_swiglu_hidden(h, wg_ref[j], wu_ref[j]) * w_e).astype(BF16))
    w_down = wdn_ref[...].reshape(per_step * EXPERT_DIM, wdn_ref.shape[-1])
    acc_ref[...] += _dot(jnp.concatenate(acts, axis=1), w_down)

    @pl.when(grp == pl.num_programs(2) - 1)
    def _():
        y = x_ref[...] + gf_ref[...] * acc_ref[...]
        if final_norm:
            y = y * lax.rsqrt(jnp.mean(y * y, axis=-1, keepdims=True) + RMS_EPS) * fg_ref[...]
        o_ref[...] = y


EXPERTS_PER_STEP = 8


def _experts(h, wd, x, gate, layer, weights, final_g, final_norm):
    bsz, seq, d = x.shape
    tm = min(seq, 1024)
    eps = EXPERTS_PER_STEP
    w_gate, w_up, w_down, sh_gate, sh_up, sh_down = weights
    tok = lambda w: pl.BlockSpec((None, tm, w), lambda b, i, e: (b, i, 0))
    routed = lambda r, c: pl.BlockSpec((None, eps, r, c), lambda b, i, e: (layer, e, 0, 0))
    shared = lambda r, c: pl.BlockSpec((None, r, c), lambda b, i, e: (layer, 0, 0))
    return pl.pallas_call(
        functools.partial(_experts_kernel, final_norm=final_norm),
        grid=(bsz, seq // tm, N_EXPERTS // eps),
        in_specs=[tok(d), tok(N_EXPERTS), tok(d),
                  pl.BlockSpec((None, 1, d), lambda b, i, e: (b, 0, 0)),
                  routed(d, EXPERT_DIM), routed(d, EXPERT_DIM), routed(EXPERT_DIM, d),
                  shared(d, EXPERT_DIM), shared(d, EXPERT_DIM), shared(EXPERT_DIM, d),
                  pl.BlockSpec((1, d), lambda b, i, e: (0, 0))],
        out_specs=tok(d),
        out_shape=jax.ShapeDtypeStruct(x.shape, F32),
        compiler_params=_params("parallel", "parallel", "arbitrary"),
        name="moe_experts",
    )(h, wd, x, gate, w_gate, w_up, w_down, sh_gate, sh_up, sh_down, final_g.reshape(1, d))


def _pad_w_in(w):
    cuts = (0, 1056, 1984, 3016, 4056)
    widths = (WA, WB, WC, WD)
    parts = []
    for lo, hi, wd in zip(cuts[:-1], cuts[1:], widths):
        parts.append(jnp.pad(w[:, lo:hi], ((0, 0), (0, wd - (hi - lo)))))
    return jnp.concatenate(parts, axis=1).astype(BF16)


def kernel(x, c, ctx, c_ctx, norm1_g, norm2_g, w_mod, b_mod, w_in, w_out, gla_gate_w2, gla_gate_b, gla_norm_g, rwkv_w_w2, rwkv_w0, rwkv_a_w2, rwkv_a0, rwkv_g_w2, rwkv_k_k, rwkv_k_a, rwkv_r_k, rwkv_ln_g, rwkv_ln_b, ssm_conv_w, ssm_conv_b, ssm_A_log, ssm_dt_bias, ssm_D, ssm_norm_g, dn_conv_w, dn_A_log, dn_dt_bias, dn_norm_g, router_w, router_bias, exp_w_gate, exp_w_up, exp_w_down, sh_w_gate, sh_w_up, sh_w_down, final_norm_g):
    bsz, n_lat, d = x.shape
    n_ctx = ctx.shape[1]
    depth = w_in.shape[0]
    rows = n_lat // GRID_W
    x = x.astype(F32)
    ctx = ctx.astype(F32)
    c_rows = jnp.concatenate([c, c_ctx[None, :], jnp.zeros((8 - bsz - 1, d), c.dtype)], axis=0).astype(F32)
    zero_init = (jnp.zeros((bsz, 2, HW, HW), F32), jnp.zeros((bsz, 2, HW, HW), F32),
                 jnp.zeros((bsz, 2, HW, SSM_STATE), F32), jnp.zeros((bsz, 2, HW, HW), F32))
    moe_weights = tuple(w.astype(BF16) for w in (exp_w_gate, exp_w_up, exp_w_down,
                                                  sh_w_gate, sh_w_up, sh_w_down))
    for i in range(depth):
        last = i == depth - 1
        lp = dict(gla_gate_w2=gla_gate_w2[i], gla_gate_b=gla_gate_b[i], rwkv_w_w2=rwkv_w_w2[i],
                  rwkv_w0=rwkv_w0[i], rwkv_a_w2=rwkv_a_w2[i], rwkv_a0=rwkv_a0[i],
                  rwkv_g_w2=rwkv_g_w2[i], rwkv_k_k=rwkv_k_k[i], rwkv_k_a=rwkv_k_a[i],
                  ssm_A_log=ssm_A_log[i], ssm_dt_bias=ssm_dt_bias[i],
                  dn_A_log=dn_A_log[i], dn_dt_bias=dn_dt_bias[i])
        mp = _mixer_params(lp)
        vecs = jnp.stack([gla_norm_g[i], rwkv_a0[i], rwkv_k_a[i], rwkv_r_k[i].reshape(GROUP),
                          rwkv_ln_g[i], rwkv_ln_b[i], jnp.repeat(ssm_D[i], HEAD_DIM), ssm_norm_g[i],
                          dn_norm_g[i]] + [jnp.zeros((GROUP,), F32)] * 7).astype(F32)
        w_in_pad = _pad_w_in(w_in[i])
        w_out_b = w_out[i].astype(BF16)
        dn_conv_b = jnp.zeros((3 * GROUP,), F32)

        mod = _modulation(c_rows, w_mod[i].astype(F32), b_mod[i].astype(F32))
        mod_x = [m[:, None, :] for m in jnp.split(mod[:bsz], 6, axis=-1)]
        mod_c = [jnp.broadcast_to(m[None], (bsz, 1, d)) for m in jnp.split(mod[bsz:bsz + 1], 6, axis=-1)]

        def mix(tokens, mods, init, g_rows, g_cols):
            proj = _in_projection(tokens, norm1_g[i], mods[0], mods[1], w_in_pad)
            convs = (_grid_conv(proj[2], GROUP, ssm_conv_w[i], ssm_conv_b[i], g_rows, g_cols),
                     _grid_conv(proj[3], 0, dn_conv_w[i], dn_conv_b, g_rows, g_cols))
            outs, states = _token_mixers(proj, convs, mp, init)
            return proj, convs, outs, states

        proj_c, convs_c, outs_c, ctx_states = mix(ctx, mod_c, zero_init, 1, n_ctx)
        proj_x, convs_x, outs_x, _ = mix(x, mod_x, ctx_states, rows, GRID_W)
        x = _out_projection(x, mod_x[2], outs_x, proj_x, convs_x[0][0], vecs,
                            mp['rwkv_wa'], mp['rwkv_wg'], w_out_b)

        def ffn(tokens, mods, final, flatten=False):
            shape = tokens.shape
            if flatten:
                tokens = tokens.reshape(1, shape[0] * shape[1], d)
                mods = [m[:1] for m in mods]
            h, wd = _route(tokens, norm2_g[i], mods[3], mods[4], router_w[i].astype(BF16),
                           router_bias[i].astype(F32))
            out = _experts(h, wd, tokens, mods[5], i, moe_weights, final_norm_g.astype(F32), final)
            return out.reshape(shape)

        x = ffn(x, mod_x, last)
        if not last:
            ctx = _out_projection(ctx, mod_c[2], outs_c, proj_c, convs_c[0][0], vecs,
                                  mp['rwkv_wa'], mp['rwkv_wg'], w_out_b)
            ctx = ffn(ctx, mod_c, False, flatten=True)
    return x
```
